```python
import math
import jax
import jax.numpy as jnp
from jax import lax
import numpy as np

D_MODEL = 2048
BATCH = 8
SEQ = 2048
DEPTH = 2
DEC_BATCH = 32
DEC_SEQ = 1
PAST_LEN = 8192
PAGE_SIZE = 128

NORM_EPS = 1e-6
ROPE_THETA = 500000.0
ROPE_FRACTION = 4
D_FF = 5632

MOBA_HEADS = 4
MOBA_HD = 128
MOBA_BLOCK = 256
MOBA_TOPK = 3
MOBA_Q_CHUNK = 16

DIFF_HEADS = 4
DIFF_HD = 64
DIFF_Q_CHUNK = 128

GDN_HEADS = 4
GDN_HD = 128
GDN_CHUNK = 64

SSM_HEADS = 8
SSM_HD = 64
SSM_GROUPS = 2
SSM_STATE = 128
SSM_CHUNK = 64

CONV_W = 4

MOBA_W = MOBA_HEADS * MOBA_HD
DIFF_QK_W = DIFF_HEADS * 2 * DIFF_HD
DIFF_V_W = DIFF_HEADS * 2 * DIFF_HD
GDN_W = GDN_HEADS * GDN_HD
SSM_W = SSM_HEADS * SSM_HD
SSM_BC_W = SSM_GROUPS * SSM_STATE
CONV_CH = 3 * GDN_W + SSM_W + 2 * SSM_BC_W
N_BRANCH = 4
BRANCH_W = 512
IN_SPLITS = (3 * MOBA_W, 2 * DIFF_QK_W + DIFF_V_W, CONV_CH, GDN_W, GDN_HEADS, GDN_HEADS,
             SSM_W, SSM_HEADS, N_BRANCH * D_MODEL)
D_IN = sum(IN_SPLITS)

kernel_name = 'hybrid_gated_branch_decoder_step'

F32 = jnp.float32


def split_cols(u, sizes):
    return jnp.split(u, np.cumsum(np.array(sizes))[:-1].tolist(), axis=-1)


def rms_norm(x, g):
    xf = x.astype(F32)
    y = xf * lax.rsqrt(jnp.mean(xf * xf, axis=-1, keepdims=True) + NORM_EPS)
    return (y * g.astype(F32)).astype(x.dtype)


def l2_normalize(x):
    xf = x.astype(F32)
    return xf * lax.rsqrt(jnp.sum(xf * xf, axis=-1, keepdims=True) + 1e-6)


def swiglu(x, w_gu, w_down):
    gate, up = jnp.split(x @ w_gu, 2, axis=-1)
    return (jax.nn.silu(gate) * up) @ w_down


def rope_partial(x, pos):
    hd = x.shape[-1]
    rd = hd // ROPE_FRACTION
    half = rd // 2
    inv_freq = 1.0 / (ROPE_THETA ** (jnp.arange(half, dtype=F32) * (2.0 / rd)))
    ang = pos.astype(F32)[:, None] * inv_freq[None, :]
    bshape = (1, x.shape[1]) + (1,) * (x.ndim - 3) + (half,)
    cos = jnp.cos(ang).reshape(bshape)
    sin = jnp.sin(ang).reshape(bshape)
    xf = x.astype(F32)
    x1 = xf[..., :half]
    x2 = xf[..., half:rd]
    out = jnp.concatenate([x1 * cos - x2 * sin, x2 * cos + x1 * sin, xf[..., rd:]], axis=-1)
    return out.astype(x.dtype)


def causal_conv(u, buf, w, b):
    T = u.shape[1]
    cat = jnp.concatenate([buf.astype(u.dtype), u], axis=1)
    out = b
    for j in range(CONV_W):
        out = out + w[j] * cat[:, j:j + T]
    return out, cat[:, T:]


def to_chunks(a, size):
    B, T, H = a.shape[:3]
    a = a.astype(F32).reshape((B, T // size, size, H) + a.shape[3:])
    return jnp.moveaxis(a, 3, 1)


def from_chunks(a):
    B, H, N, C = a.shape[:4]
    return jnp.moveaxis(a, 1, 3).reshape((B, N * C, H) + a.shape[4:])


def moba_attention(q, k, v, q_start):
    B, Q, H, hd = q.shape
    Tk = k.shape[1]
    nb = -(-Tk // MOBA_BLOCK)
    pad = nb * MOBA_BLOCK - Tk
    kb = jnp.pad(k, ((0, 0), (0, pad), (0, 0), (0, 0))).reshape(B, nb, MOBA_BLOCK, H, hd)
    vb = jnp.pad(v, ((0, 0), (0, pad), (0, 0), (0, 0))).reshape(B, nb, MOBA_BLOCK, H, hd)
    k_mean = jnp.mean(kb.astype(F32), axis=2)
    qpos = q_start + jnp.arange(Q, dtype=jnp.int32)
    qblk = qpos // MOBA_BLOCK
    qf = q.astype(F32).transpose(0, 2, 1, 3)
    gate = jnp.einsum('bhqd,bnhd->bhqn', qf, k_mean)
    fully_past = jnp.arange(nb)[None, :] < qblk[:, None]
    gate = jnp.where(fully_past, gate, -jnp.inf)
    n_sel = min(MOBA_TOPK, nb)
    top_val, top_idx = lax.top_k(gate, n_sel)
    own = jnp.broadcast_to(qblk[None, None, :, None], (B, H, Q, 1)).astype(top_idx.dtype)
    blk_idx = jnp.concatenate([top_idx, own], axis=-1)
    blk_ok = jnp.concatenate([jnp.isfinite(top_val), jnp.ones((B, H, Q, 1), bool)], axis=-1)
    qc = math.gcd(Q, MOBA_Q_CHUNK)
    nc = Q // qc

    def chunked(a):
        return jnp.moveaxis(a.reshape(a.shape[:2] + (nc, qc) + a.shape[3:]), 2, 0)

    b_ix = jnp.arange(B)[:, None, None, None]
    h_ix = jnp.arange(H)[None, :, None, None]
    offs = jnp.arange(MOBA_BLOCK)
    scale = hd ** -0.5

    def one_chunk(args):
        qq, idx, ok, pos = args
        kg = kb[b_ix, idx, :, h_ix].astype(F32)
        vg = vb[b_ix, idx, :, h_ix].astype(F32)
        s = jnp.einsum('bhqd,bhqskd->bhqsk', qq, kg) * scale
        kpos = idx[..., None] * MOBA_BLOCK + offs
        valid = ok[..., None] & (kpos <= pos[None, None, :, None, None])
        s = jnp.where(valid, s, -jnp.inf)
        p = jax.nn.softmax(s.reshape(s.shape[:3] + (-1,)), axis=-1).reshape(s.shape)
        return jnp.einsum('bhqsk,bhqskd->bhqd', p, vg)

    out = lax.map(one_chunk, (chunked(qf), chunked(blk_idx), chunked(blk_ok), qpos.reshape(nc, qc)))
    out = jnp.moveaxis(out, 0, 2).reshape(B, H, Q, hd).transpose(0, 2, 1, 3)
    return out.astype(q.dtype)


def diff_attention(q, k, v, q_start, lam):
    B, Q, H, _, d = q.shape
    Tk = k.shape[1]
    qc = math.gcd(Q, DIFF_Q_CHUNK)
    nc = Q // qc
    kf = k.astype(F32)
    vf = v.astype(F32)
    kpos = jnp.arange(Tk)
    q_blocks = jnp.moveaxis(q.astype(F32).reshape(B, nc, qc, H, 2, d), 1, 0)
    p_blocks = (q_start + jnp.arange(Q, dtype=jnp.int32)).reshape(nc, qc)
    scale = d ** -0.5

    def one_block(args):
        qb, pos = args
        s = jnp.einsum('bqhcd,bkhcd->bhcqk', qb, kf) * scale
        s = jnp.where(kpos[None, :] <= pos[:, None], s, -jnp.inf)
        p = jax.nn.softmax(s, axis=-1)
        a = p[:, :, 0] - lam * p[:, :, 1]
        return jnp.einsum('bhqk,bkhe->bqhe', a, vf)

    out = lax.map(one_block, (q_blocks, p_blocks))
    return jnp.moveaxis(out, 0, 1).reshape(B, Q, H, 2 * d).astype(q.dtype)


def gdn_chunked(q, k, v, g, beta, s0):
    dk = q.shape[-1]
    dv = v.shape[-1]
    C = math.gcd(q.shape[1], GDN_CHUNK)
    qc = to_chunks(q, C) * dk ** -0.5
    kc = to_chunks(k, C)
    vc = to_chunks(v, C)
    bc = to_chunks(beta, C)
    gc = jnp.cumsum(to_chunks(g, C), axis=-1)
    incl = jnp.tril(jnp.ones((C, C), bool))
    strict = jnp.tril(jnp.ones((C, C), bool), -1)
    decay = jnp.exp(jnp.where(incl, gc[..., :, None] - gc[..., None, :], -jnp.inf))
    kb = kc * bc[..., None]
    lmat = jnp.where(strict, jnp.einsum('bhnid,bhnjd->bhnij', kb, kc) * decay, 0.0)
    rhs = jnp.concatenate([vc * bc[..., None], kb * jnp.exp(gc)[..., None]], axis=-1)
    sol = lax.linalg.triangular_solve(jnp.eye(C, dtype=F32) + lmat, rhs, left_side=True, lower=True)
    u, w = sol[..., :dv], sol[..., dv:]
    a_intra = jnp.einsum('bhnid,bhnjd->bhnij', qc, kc) * decay
    q_dec = qc * jnp.exp(gc)[..., None]
    k_dec = kc * jnp.exp(gc[..., -1:] - gc)[..., None]
    g_last = jnp.exp(gc[..., -1])

    def step(S, xs):
        u_n, w_n, a_n, qd_n, kd_n, gl_n = xs
        v_new = u_n - jnp.einsum('bhcd,bhde->bhce', w_n, S)
        o = jnp.einsum('bhcd,bhde->bhce', qd_n, S) + jnp.einsum('bhij,bhje->bhie', a_n, v_new)
        S = S * gl_n[..., None, None] + jnp.einsum('bhcd,bhce->bhde', kd_n, v_new)
        return S, o

    xs = tuple(jnp.moveaxis(a, 2, 0) for a in (u, w, a_intra, q_dec, k_dec, g_last))
    s_last, o = lax.scan(step, s0.astype(F32), xs)
    return from_chunks(jnp.moveaxis(o, 0, 2)), s_last


def ssd_chunked(x, dt, A, Bm, Cm, h0):
    H = x.shape[2]
    rep = H // Bm.shape[2]
    C = math.gcd(x.shape[1], SSM_CHUNK)
    xc = to_chunks(x, C)
    dtc = to_chunks(dt, C)
    Bc = to_chunks(jnp.repeat(Bm, rep, axis=2), C)
    Cc = to_chunks(jnp.repeat(Cm, rep, axis=2), C)
    ac = jnp.cumsum(dtc * A[None, :, None, None], axis=-1)
    incl = jnp.tril(jnp.ones((C, C), bool))
    lmask = jnp.exp(jnp.where(incl, ac[..., :, None] - ac[..., None, :], -jnp.inf))
    xdt = xc * dtc[..., None]
    y_intra = jnp.einsum('bhnij,bhnjp->bhnip', jnp.einsum('bhnis,bhnjs->bhnij', Cc, Bc) * lmask, xdt)
    c_dec = Cc * jnp.exp(ac)[..., None]
    x_dec = xdt * jnp.exp(ac[..., -1:] - ac)[..., None]
    a_last = jnp.exp(ac[..., -1])

    def step(h, xs):
        cd, xd, bn, al = xs
        y = jnp.einsum('bhcs,bhps->bhcp', cd, h)
        h = h * al[..., None, None] + jnp.einsum('bhcp,bhcs->bhps', xd, bn)
        return h, y

    xs = tuple(jnp.moveaxis(a, 2, 0) for a in (c_dec, x_dec, Bc, a_last))
    h_last, y_inter = lax.scan(step, h0.astype(F32), xs)
    return from_chunks(y_intra + jnp.moveaxis(y_inter, 0, 2)), h_last


def trunk_layer(h, q_start, past_moba, past_diff, s_gdn, h_ssm, conv_buf, l, lw):
    B, T, _ = h.shape
    pos = q_start + jnp.arange(T, dtype=jnp.int32)
    h = h + 0.5 * swiglu(rms_norm(h, lw['ffn1_norm']), lw['ffn1_w_gu'], lw['ffn1_w_down'])

    u = rms_norm(h, lw['mix_norm'])
    proj = u @ lw['w_in']
    (moba_qkv, diff_qkv, conv_in, gdn_z, gdn_b, gdn_a, ssm_z, ssm_dt, gate_logits) = split_cols(proj, IN_SPLITS)

    mq, mk, mv = [t.reshape(B, T, MOBA_HEADS, MOBA_HD) for t in jnp.split(moba_qkv, 3, axis=-1)]
    mq = rope_partial(mq, pos)
    mk = rope_partial(mk, pos)
    moba_rows = jnp.stack([mk, mv], axis=2)
    moba_all = moba_rows if past_moba is None else jnp.concatenate([past_moba.astype(moba_rows.dtype), moba_rows], axis=1)
    o_moba = moba_attention(mq, moba_all[:, :, 0], moba_all[:, :, 1], q_start)

    dq, dkk, dv = jnp.split(diff_qkv, [DIFF_QK_W, 2 * DIFF_QK_W], axis=-1)
    dq = rope_partial(dq.reshape(B, T, DIFF_HEADS, 2, DIFF_HD), pos)
    dkk = rope_partial(dkk.reshape(B, T, DIFF_HEADS, 2, DIFF_HD), pos)
    diff_rows = jnp.stack([dkk.reshape(B, T, DIFF_HEADS, 2 * DIFF_HD), dv.reshape(B, T, DIFF_HEADS, 2 * DIFF_HD)], axis=2)
    diff_all = diff_rows if past_diff is None else jnp.concatenate([past_diff.astype(diff_rows.dtype), diff_rows], axis=1)
    Tk = diff_all.shape[1]
    lam_init = 0.8 - 0.6 * math.exp(-0.3 * l)
    lp = lw['diff_lambda'].astype(F32)
    lam = jnp.exp(jnp.sum(lp[0] * lp[1])) - jnp.exp(jnp.sum(lp[2] * lp[3])) + lam_init
    o_diff = diff_attention(dq, diff_all[:, :, 0].reshape(B, Tk, DIFF_HEADS, 2, DIFF_HD), diff_all[:, :, 1], q_start, lam)
    o_diff = rms_norm(o_diff, lw['diff_norm']) * (1.0 - lam_init)

    conv_out, conv_new = causal_conv(conv_in, conv_buf, lw['conv_w'], lw['conv_b'])
    conv_out = jax.nn.silu(conv_out)
    gq, gk, gv, sx, sB, sC = split_cols(conv_out, (GDN_W, GDN_W, GDN_W, SSM_W, SSM_BC_W, SSM_BC_W))

    gq = l2_normalize(gq.reshape(B, T, GDN_HEADS, GDN_HD))
    gk = l2_normalize(gk.reshape(B, T, GDN_HEADS, GDN_HD))
    gv = gv.reshape(B, T, GDN_HEADS, GDN_HD)
    beta = jax.nn.sigmoid(gdn_b.astype(F32))
    g = -jnp.exp(lw['gdn_A_log'].astype(F32)) * jax.nn.softplus(gdn_a.astype(F32) + lw['gdn_dt_bias'].astype(F32))
    o_gdn, s_gdn_new = gdn_chunked(gq, gk, gv, g, beta, s_gdn)
    o_gdn = rms_norm(o_gdn, lw['gdn_norm']) * jax.nn.silu(gdn_z.reshape(B, T, GDN_HEADS, GDN_HD).astype(F32))

    dt = jax.nn.softplus(ssm_dt.astype(F32) + lw['ssm_dt_bias'].astype(F32))
    A = -jnp.exp(lw['ssm_A_log'].astype(F32))
    xs = sx.reshape(B, T, SSM_HEADS, SSM_HD)
    y_ssm, h_ssm_new = ssd_chunked(xs, dt, A, sB.reshape(B, T, SSM_GROUPS, SSM_STATE), sC.reshape(B, T, SSM_GROUPS, SSM_STATE), h_ssm)
    y_ssm = y_ssm + lw['ssm_D'].astype(F32)[:, None] * xs.astype(F32)
    y_ssm = (y_ssm.reshape(B, T, SSM_W) * jax.nn.silu(ssm_z.astype(F32))).reshape(B, T, SSM_GROUPS, SSM_W // SSM_GROUPS)
    y_ssm = rms_norm(y_ssm, lw['ssm_norm'].reshape(SSM_GROUPS, SSM_W // SSM_GROUPS))

    branches = (o_moba.reshape(B, T, BRANCH_W), o_diff.reshape(B, T, BRANCH_W),
                o_gdn.reshape(B, T, BRANCH_W), y_ssm.reshape(B, T, BRANCH_W))
    gates = jax.nn.sigmoid(gate_logits.reshape(B, T, N_BRANCH, D_MODEL))
    merged = jnp.zeros_like(h)
    for i in range(N_BRANCH):
        merged = merged + gates[:, :, i] * (branches[i].astype(h.dtype) @ lw['w_branch'][i])
    h = h + merged @ lw['w_out']

    h = h + 0.5 * swiglu(rms_norm(h, lw['ffn2_norm']), lw['ffn2_w_gu'], lw['ffn2_w_down'])
    return h, moba_rows, diff_rows, s_gdn_new, h_ssm_new, conv_new


def setup_inputs(seed: int = 0) -> dict:
    key = jax.random.key(seed)
    ks = list(jax.random.split(key, 32))

    def normal(i, shape, scale=1.0):
        return jax.random.normal(ks[i], shape, F32) * scale

    def gain(i, shape):
        return 1.0 + normal(i, shape, 0.02)

    def log_decay_rate(i, n):
        return jnp.log(jax.random.uniform(ks[i], (DEPTH, n), F32, 1.0, 16.0))

    def dt_bias(i, n):
        dt = jnp.exp(jax.random.uniform(ks[i], (DEPTH, n), F32, math.log(1e-3), math.log(1e-1)))
        return dt + jnp.log(-jnp.expm1(-dt))

    n_pages = PAST_LEN // PAGE_SIZE
    n_used = DEC_BATCH * n_pages
    n_phys = n_used + max(1, n_used // 4)
    page_table = jax.random.permutation(ks[0], n_phys)[:n_used].reshape(DEC_BATCH, n_pages).astype(jnp.int32)
    d = D_MODEL
    return {
        'x_prompt': normal(1, (BATCH, SEQ, d)),
        'x_sample': normal(2, (DEC_BATCH, DEC_SEQ, d)),
        'cache_moba_kv': normal(3, (n_phys, DEPTH, PAGE_SIZE, 2, MOBA_HEADS, MOBA_HD)),
        'cache_diff_kv': normal(4, (n_phys, DEPTH, PAGE_SIZE, 2, DIFF_HEADS, 2 * DIFF_HD)),
        'state_gdn': normal(5, (DEPTH, DEC_BATCH, GDN_HEADS, GDN_HD, GDN_HD), 0.1),
        'state_ssm': normal(6, (DEPTH, DEC_BATCH, SSM_HEADS, SSM_HD, SSM_STATE), 0.1),
        'state_conv': normal(7, (DEPTH, DEC_BATCH, CONV_W - 1, CONV_CH)),
        'page_table': page_table,
        'ffn1_norm': gain(8, (DEPTH, d)),
        'ffn1_w_gu': normal(9, (DEPTH, d, 2 * D_FF), d ** -0.5),
        'ffn1_w_down': normal(10, (DEPTH, D_FF, d), D_FF ** -0.5),
        'mix_norm': gain(11, (DEPTH, d)),
        'w_in': normal(12, (DEPTH, d, D_IN), d ** -0.5),
        'conv_w': normal(13, (DEPTH, CONV_W, CONV_CH), CONV_W ** -0.5),
        'conv_b': normal(14, (DEPTH, CONV_CH), 0.01),
        'diff_lambda': normal(15, (DEPTH, 4, DIFF_HD), 0.1),
        'diff_norm': gain(16, (DEPTH, 2 * DIFF_HD)),
        'gdn_A_log': log_decay_rate(17, GDN_HEADS),
        'gdn_dt_bias': dt_bias(18, GDN_HEADS),
        'gdn_norm': gain(19, (DEPTH, GDN_HD)),
        'ssm_A_log': log_decay_rate(20, SSM_HEADS),
        'ssm_dt_bias': dt_bias(21, SSM_HEADS),
        'ssm_D': 1.0 + normal(22, (DEPTH, SSM_HEADS), 0.1),
        'ssm_norm': gain(23, (DEPTH, SSM_W)),
        'w_branch': normal(24, (DEPTH, N_BRANCH, BRANCH_W, d), BRANCH_W ** -0.5),
        'w_out': normal(25, (DEPTH, d, d), d ** -0.5),
        'ffn2_norm': gain(26, (DEPTH, d)),
        'ffn2_w_gu': normal(27, (DEPTH, d, 2 * D_FF), d ** -0.5),
        'ffn2_w_down': normal(28, (DEPTH, D_FF, d), D_FF ** -0.5),
        'final_norm': gain(29, (d,)),
    }


def reference(x_prompt, x_sample, cache_moba_kv, cache_diff_kv, state_gdn, state_ssm, state_conv, page_table,
              ffn1_norm, ffn1_w_gu, ffn1_w_down, mix_norm, w_in, conv_w, conv_b, diff_lambda, diff_norm,
              gdn_A_log, gdn_dt_bias, gdn_norm, ssm_A_log, ssm_dt_bias, ssm_D, ssm_norm, w_branch, w_out,
              ffn2_norm, ffn2_w_gu, ffn2_w_down, final_norm):
    Bp = x_prompt.shape[0]
    Bs = x_sample.shape[0]
    past_len = page_table.shape[1] * PAGE_SIZE
    hp, hs = x_prompt, x_sample
    moba_p, moba_s, diff_p, diff_s = [], [], [], []
    gdn_p, gdn_s, ssm_p, ssm_s, conv_p, conv_s = [], [], [], [], [], []
    for l in range(DEPTH):
        lw = dict(ffn1_norm=ffn1_norm[l], ffn1_w_gu=ffn1_w_gu[l], ffn1_w_down=ffn1_w_down[l],
                  mix_norm=mix_norm[l], w_in=w_in[l], conv_w=conv_w[l], conv_b=conv_b[l],
                  diff_lambda=diff_lambda[l], diff_norm=diff_norm[l], gdn_A_log=gdn_A_log[l],
                  gdn_dt_bias=gdn_dt_bias[l], gdn_norm=gdn_norm[l], ssm_A_log=ssm_A_log[l],
                  ssm_dt_bias=ssm_dt_bias[l], ssm_D=ssm_D[l], ssm_norm=ssm_norm[l],
                  w_branch=w_branch[l], w_out=w_out[l], ffn2_norm=ffn2_norm[l],
                  ffn2_w_gu=ffn2_w_gu[l], ffn2_w_down=ffn2_w_down[l])
        hp, mr, dr, sg, sh, cb = trunk_layer(
            hp, 0, None, None,
            jnp.zeros((Bp, GDN_HEADS, GDN_HD, GDN_HD), F32),
            jnp.zeros((Bp, SSM_HEADS, SSM_HD, SSM_STATE), F32),
            jnp.zeros((Bp, CONV_W - 1, CONV_CH), x_prompt.dtype), l, lw)
        moba_p.append(mr); diff_p.append(dr); gdn_p.append(sg); ssm_p.append(sh); conv_p.append(cb)
        past_m = cache_moba_kv[page_table, l].reshape(Bs, past_len, 2, MOBA_HEADS, MOBA_HD)
        past_d = cache_diff_kv[page_table, l].reshape(Bs, past_len, 2, DIFF_HEADS, 2 * DIFF_HD)
        hs, mr, dr, sg, sh, cb = trunk_layer(
            hs, PAST_LEN, past_m, past_d, state_gdn[l], state_ssm[l], state_conv[l], l, lw)
        moba_s.append(mr); diff_s.append(dr); gdn_s.append(sg); ssm_s.append(sh); conv_s.append(cb)
    y_prompt = rms_norm(hp, final_norm)
    y_sample = rms_norm(hs, final_norm)
    return (y_prompt, y_sample,
            jnp.stack(moba_p, axis=1), jnp.stack(moba_s, axis=1),
            jnp.stack(diff_p, axis=1), jnp.stack(diff_s, axis=1),
            jnp.stack(gdn_p), jnp.stack(gdn_s),
            jnp.stack(ssm_p), jnp.stack(ssm_s),
            jnp.stack(conv_p), jnp.stack(conv_s))
```

```python
import functools
import math

import jax
import jax.numpy as jnp
from jax import lax
from jax.experimental import pallas as pl
from jax.experimental.pallas import tpu as pltpu

F32 = jnp.float32
BF16 = jnp.bfloat16

D_MODEL = 2048
D_FF = 5632
NORM_EPS = 1e-6
ROPE_THETA = 500000.0
PAGE_SIZE = 128

MOBA_HEADS = 4
MOBA_HD = 128
MOBA_BLOCK = 256
MOBA_TOPK = 3
DIFF_HEADS = 4
DIFF_HD = 64
GDN_HEADS = 4
GDN_HD = 128
SSM_HEADS = 8
SSM_HD = 64
SSM_GROUPS = 2
SSM_STATE = 128
CHUNK = 64
CONV_W = 4
HEAD_W = 512
CONV_CH = 2560
N_BRANCH = 4

LANES = 128
V7X_VMEM_LIMIT = 56 * 1024 * 1024

_C_MOBA = 0
_C_DIFF = 1536
_C_CONV = 3072
_C_GDNZ = 5632
_C_GDNB = 6144
_C_GDNA = 6148
_C_SSMZ = 6152
_C_SSMDT = 6664
_C_GATE = 6672


def _cparams(*sem):
    return pltpu.CompilerParams(dimension_semantics=sem, vmem_limit_bytes=V7X_VMEM_LIMIT)


def _rms(x, gain):
    y = x * lax.rsqrt(jnp.mean(x * x, axis=-1, keepdims=True) + NORM_EPS)
    return y * gain


def _dot(a, b):
    return jnp.dot(a, b, preferred_element_type=F32)


def _dot_nt(a, b):
    return lax.dot_general(a, b, (((1,), (1,)), ((), ())), preferred_element_type=F32)


def _dot_tn(a, b):
    return lax.dot_general(a, b, (((0,), (0,)), ((), ())), preferred_element_type=F32)


def _softplus(x):
    return jnp.maximum(x, 0.0) + jnp.log(1.0 + jnp.exp(-jnp.abs(x)))


def _sigmoid(x):
    return 1.0 / (1.0 + jnp.exp(-x))


def _silu(x):
    return x * _sigmoid(x)


def _ffn_kernel(x_ref, g_ref, wg_ref, wu_ref, wd_ref, pg_ref, o_ref, p_ref, xn_ref):
    f = pl.program_id(1)

    @pl.when(f == 0)
    def _():
        x = x_ref[...]
        xn_ref[...] = _rms(x, g_ref[...]).astype(BF16)
        o_ref[...] = x

    xn = xn_ref[...]
    gate = _dot(xn, wg_ref[...])
    up = _dot(xn, wu_ref[...])
    act = (0.5 * _silu(gate) * up).astype(BF16)
    o_ref[...] += _dot(act, wd_ref[...])

    @pl.when(f == pl.num_programs(1) - 1)
    def _():
        p_ref[...] = _rms(o_ref[...], pg_ref[...]).astype(p_ref.dtype)


def _ffn(x, gain, w_gu, w_down, post_gain, post_dtype, tm, tf=512):
    m = x.shape[0]
    nf = D_FF // tf
    return pl.pallas_call(
        _ffn_kernel,
        grid=(m // tm, nf),
        in_specs=[
            pl.BlockSpec((tm, D_MODEL), lambda i, f: (i, 0)),
            pl.BlockSpec((1, D_MODEL), lambda i, f: (0, 0)),
            pl.BlockSpec((D_MODEL, tf), lambda i, f: (0, f)),
            pl.BlockSpec((D_MODEL, tf), lambda i, f: (0, f + nf)),
            pl.BlockSpec((tf, D_MODEL), lambda i, f: (f, 0)),
            pl.BlockSpec((1, D_MODEL), lambda i, f: (0, 0)),
        ],
        out_specs=[
            pl.BlockSpec((tm, D_MODEL), lambda i, f: (i, 0)),
            pl.BlockSpec((tm, D_MODEL), lambda i, f: (i, 0)),
        ],
        out_shape=[
            jax.ShapeDtypeStruct((m, D_MODEL), F32),
            jax.ShapeDtypeStruct((m, D_MODEL), post_dtype),
        ],
        scratch_shapes=[pltpu.VMEM((tm, D_MODEL), BF16)],
        compiler_params=_cparams("parallel", "arbitrary"),
        name="ffn",
    )(x, gain.reshape(1, D_MODEL), w_gu, w_gu, w_down, post_gain.reshape(1, D_MODEL))


def _rope(x, c_ref, s1_ref, s2_ref, half):
    n = x.shape[1]
    reps = n // LANES
    c = jnp.concatenate([c_ref[...]] * reps, axis=1)
    s1 = jnp.concatenate([s1_ref[...]] * reps, axis=1)
    s2 = jnp.concatenate([s2_ref[...]] * reps, axis=1)
    return x * c + pltpu.roll(x, half, 1) * s1 + pltpu.roll(x, n - half, 1) * s2


def _qkv_kernel(u_ref, w_ref, mc_ref, ms1_ref, ms2_ref, dc_ref, ds1_ref, ds2_ref, o_ref):
    j = pl.program_id(1)
    acc = _dot(u_ref[...], w_ref[...])

    @pl.when(j < 2)
    def _():
        o_ref[...] = _rope(acc, mc_ref, ms1_ref, ms2_ref, MOBA_HD // 8)

    @pl.when((j == 3) | (j == 4))
    def _():
        o_ref[...] = _rope(acc, dc_ref, ds1_ref, ds2_ref, DIFF_HD // 8)

    @pl.when((j == 2) | (j == 5))
    def _():
        o_ref[...] = acc


def _qkv_proj(u, w, tables, tm, rows_per_seq):
    m = u.shape[0]
    tn = HEAD_W
    nt = rows_per_seq // tm if rows_per_seq >= tm else 1
    tspec = pl.BlockSpec((tm, LANES), lambda i, j: (i % nt, 0))
    return pl.pallas_call(
        _qkv_kernel,
        grid=(m // tm, w.shape[1] // tn),
        in_specs=[
            pl.BlockSpec((tm, D_MODEL), lambda i, j: (i, 0)),
            pl.BlockSpec((D_MODEL, tn), lambda i, j: (0, j)),
        ] + [tspec] * 6,
        out_specs=pl.BlockSpec((tm, tn), lambda i, j: (i, j)),
        out_shape=jax.ShapeDtypeStruct((m, w.shape[1]), F32),
        compiler_params=_cparams("parallel", "arbitrary"),
        name="qkv_proj",
    )(u, w, *tables)


def _mm_kernel(u_ref, w_ref, o_ref):
    o_ref[...] = _dot(u_ref[...], w_ref[...]).astype(o_ref.dtype)


def _matmul(u, w, tm, tn, out_dtype=F32):
    m, k = u.shape
    n = w.shape[1]
    return pl.pallas_call(
        _mm_kernel,
        grid=(m // tm, n // tn),
        in_specs=[
            pl.BlockSpec((tm, k), lambda i, j: (i, 0)),
            pl.BlockSpec((k, tn), lambda i, j: (0, j)),
        ],
        out_specs=pl.BlockSpec((tm, tn), lambda i, j: (i, j)),
        out_shape=jax.ShapeDtypeStruct((m, n), out_dtype),
        compiler_params=_cparams("parallel", "arbitrary"),
        name="proj",
    )(u, w)


def _merge_kernel(u_ref, b0_ref, b1_ref, b2_ref, b3_ref, wg_ref, wb_ref, o_ref):
    u = u_ref[...]
    acc = None
    for i, b_ref in enumerate((b0_ref, b1_ref, b2_ref, b3_ref)):
        gate = _sigmoid(_dot(u, wg_ref[i]))
        term = gate * _dot(b_ref[...], wb_ref[i])
        acc = term if acc is None else acc + term
    o_ref[...] = acc.astype(o_ref.dtype)


def _merge(u, branches, w_gate, w_branch, tm, tn=256):
    m = u.shape[0]
    bspec = pl.BlockSpec((tm, HEAD_W), lambda i, j: (i, 0))
    return pl.pallas_call(
        _merge_kernel,
        grid=(m // tm, D_MODEL // tn),
        in_specs=[pl.BlockSpec((tm, D_MODEL), lambda i, j: (i, 0)), bspec, bspec, bspec, bspec,
                  pl.BlockSpec((N_BRANCH, D_MODEL, tn), lambda i, j: (0, 0, j)),
                  pl.BlockSpec((N_BRANCH, HEAD_W, tn), lambda i, j: (0, 0, j))],
        out_specs=pl.BlockSpec((tm, tn), lambda i, j: (i, j)),
        out_shape=jax.ShapeDtypeStruct((m, D_MODEL), BF16),
        compiler_params=_cparams("parallel", "arbitrary"),
        name="merge",
    )(u, *branches, w_gate, w_branch)


def _outproj_kernel(a_ref, w_ref, h_ref, o_ref):
    o_ref[...] = h_ref[...] + _dot(a_ref[...], w_ref[...])


def _outproj(a, w, h, tm, tn=512):
    m = a.shape[0]
    return pl.pallas_call(
        _outproj_kernel,
        grid=(m // tm, D_MODEL // tn),
        in_specs=[pl.BlockSpec((tm, D_MODEL), lambda i, j: (i, 0)),
                  pl.BlockSpec((D_MODEL, tn), lambda i, j: (0, j)),
                  pl.BlockSpec((tm, tn), lambda i, j: (i, j))],
        out_specs=pl.BlockSpec((tm, tn), lambda i, j: (i, j)),
        out_shape=jax.ShapeDtypeStruct((m, D_MODEL), F32),
        compiler_params=_cparams("parallel", "arbitrary"),
        name="outproj",
    )(a, w, h)


def _split_hi_lo(x):
    hi = x.astype(BF16)
    lo = (x - hi.astype(F32)).astype(BF16)
    return hi, lo


def _moba_kernel(q_ref, k_ref, v_ref, o_ref, kb_ref, vb_ref, km_ref, *, nb, n_sel):
    qi = pl.program_id(2)
    t = k_ref.shape[0]
    bq = q_ref.shape[0]

    @pl.when(qi == 0)
    def _():
        k = k_ref[...]
        kb_ref[...] = k.astype(BF16)
        vb_ref[...] = v_ref[...].astype(BF16)
        km_ref[...] = jnp.zeros_like(km_ref)
        for n in range(nb):
            km_ref[n:n + 1, :] = jnp.mean(k[n * MOBA_BLOCK:(n + 1) * MOBA_BLOCK, :], axis=0, keepdims=True)

    q = q_ref[...]
    qh, ql = _split_hi_lo(q)
    kh, kl = _split_hi_lo(km_ref[...])
    gate = _dot_nt(qh, kh) + (_dot_nt(qh, kl) + _dot_nt(ql, kh))
    lane = lax.broadcasted_iota(jnp.int32, gate.shape, 1)
    past = lane < qi
    gate = jnp.where(past, gate, -jnp.inf)
    rank = jnp.zeros(gate.shape, jnp.int32)
    for m in range(nb):
        gm = gate[:, m:m + 1]
        beats = (gm > gate) | ((gm == gate) & (m < lane))
        rank = rank + beats.astype(jnp.int32)
    sel = (past & (rank < n_sel)).astype(F32).astype(BF16)
    erow = lax.broadcasted_iota(jnp.int32, (LANES, t), 0)
    ecol = lax.broadcasted_iota(jnp.int32, (LANES, t), 1)
    expand = (ecol // MOBA_BLOCK == erow).astype(F32).astype(BF16)
    chosen = _dot(sel, expand) > 0.5
    row = lax.broadcasted_iota(jnp.int32, (bq, t), 0) + qi * bq
    col = lax.broadcasted_iota(jnp.int32, (bq, t), 1)
    own = (col // MOBA_BLOCK == row // MOBA_BLOCK) & (col <= row)
    s = _dot_nt(q.astype(BF16), kb_ref[...]) * (MOBA_HD ** -0.5)
    s = jnp.where(chosen | own, s, -jnp.inf)
    p = jnp.exp(s - jnp.max(s, axis=-1, keepdims=True))
    p = p / jnp.sum(p, axis=-1, keepdims=True)
    o_ref[...] = _dot(p.astype(BF16), vb_ref[...]).astype(o_ref.dtype)


def _moba_prompt(qkv, batch, t):
    bq = MOBA_BLOCK
    nq = t // bq
    nb = t // MOBA_BLOCK
    kern = functools.partial(_moba_kernel, nb=nb, n_sel=min(MOBA_TOPK, nb))
    return pl.pallas_call(
        kern,
        grid=(batch, MOBA_HEADS, nq),
        in_specs=[
            pl.BlockSpec((bq, MOBA_HD), lambda b, h, i: (b * nq + i, h)),
            pl.BlockSpec((t, MOBA_HD), lambda b, h, i: (b, MOBA_HEADS + h)),
            pl.BlockSpec((t, MOBA_HD), lambda b, h, i: (b, 2 * MOBA_HEADS + h)),
        ],
        out_specs=pl.BlockSpec((bq, MOBA_HD), lambda b, h, i: (b * nq + i, h)),
        out_shape=jax.ShapeDtypeStruct((batch * t, HEAD_W), BF16),
        scratch_shapes=[pltpu.VMEM((t, MOBA_HD), BF16), pltpu.VMEM((t, MOBA_HD), BF16),
                        pltpu.VMEM((LANES, MOBA_HD), F32)],
        compiler_params=_cparams("parallel", "parallel", "arbitrary"),
        name="moba_prompt",
    )(qkv, qkv, qkv)


def _diff_lambda(lp_ref, lam_init):
    lp = lp_ref[...]
    a = jnp.sum(lp[0:1] * lp[1:2], axis=-1, keepdims=True)
    b = jnp.sum(lp[2:3] * lp[3:4], axis=-1, keepdims=True)
    return jnp.exp(a) - jnp.exp(b) + lam_init


def _diff_kernel(q_ref, k_ref, v_ref, lp_ref, g_ref, o_ref, kb_ref, vb_ref, *, lam_init):
    qi = pl.program_id(2)
    t = k_ref.shape[0]
    bq = q_ref.shape[0]

    @pl.when(qi == 0)
    def _():
        kb_ref[...] = k_ref[...].astype(BF16)
        vb_ref[...] = v_ref[...].astype(BF16)

    lam = _diff_lambda(lp_ref, lam_init)
    q = q_ref[...]
    lane = lax.broadcasted_iota(jnp.int32, q.shape, 1)
    q1 = jnp.where(lane < DIFF_HD, q, 0.0).astype(BF16)
    q2 = jnp.where(lane >= DIFF_HD, q, 0.0).astype(BF16)
    row = lax.broadcasted_iota(jnp.int32, (bq, t), 0) + qi * bq
    col = lax.broadcasted_iota(jnp.int32, (bq, t), 1)
    causal = col <= row
    kb = kb_ref[...]

    def softmax(qc):
        s = _dot_nt(qc, kb) * (DIFF_HD ** -0.5)
        s = jnp.where(causal, s, -jnp.inf)
        p = jnp.exp(s - jnp.max(s, axis=-1, keepdims=True))
        return p / jnp.sum(p, axis=-1, keepdims=True)

    a = softmax(q1) - lam * softmax(q2)
    o = _dot(a.astype(BF16), vb_ref[...])
    o_ref[...] = (_rms(o, g_ref[...]) * (1.0 - lam_init)).astype(o_ref.dtype)


def _diff_prompt(qkv, lam_params, norm_gain, lam_init, batch, t):
    bq = 256
    nq = t // bq
    c0 = _C_DIFF // LANES
    kern = functools.partial(_diff_kernel, lam_init=lam_init)
    return pl.pallas_call(
        kern,
        grid=(batch, DIFF_HEADS, nq),
        in_specs=[
            pl.BlockSpec((bq, LANES), lambda b, h, i: (b * nq + i, c0 + h)),
            pl.BlockSpec((t, LANES), lambda b, h, i: (b, c0 + DIFF_HEADS + h)),
            pl.BlockSpec((t, LANES), lambda b, h, i: (b, c0 + 2 * DIFF_HEADS + h)),
            pl.BlockSpec((4, DIFF_HD), lambda b, h, i: (0, 0)),
            pl.BlockSpec((1, LANES), lambda b, h, i: (0, 0)),
        ],
        out_specs=pl.BlockSpec((bq, LANES), lambda b, h, i: (b * nq + i, h)),
        out_shape=jax.ShapeDtypeStruct((batch * t, HEAD_W), BF16),
        scratch_shapes=[pltpu.VMEM((t, LANES), BF16), pltpu.VMEM((t, LANES), BF16)],
        compiler_params=_cparams("parallel", "parallel", "arbitrary"),
        name="diff_prompt",
    )(qkv, qkv, qkv, lam_params, norm_gain.reshape(1, LANES))


def _l2norm_heads(x):
    parts = []
    for h in range(x.shape[1] // LANES):
        xh = x[:, h * LANES:(h + 1) * LANES]
        parts.append(xh * lax.rsqrt(jnp.sum(xh * xh, axis=-1, keepdims=True) + 1e-6))
    return jnp.concatenate(parts, axis=1)


def _conv_kernel(x_ref, w_ref, b_ref, o_ref, tail_ref):
    c = pl.program_id(1)
    x = x_ref[...]
    t = x.shape[0]
    row = lax.broadcasted_iota(jnp.int32, x.shape, 0)
    w = w_ref[...]
    acc = b_ref[...] + w[CONV_W - 1:CONV_W] * x
    for d in range(1, CONV_W):
        shifted = jnp.where(row >= d, pltpu.roll(x, d, 0), 0.0)
        acc = acc + w[CONV_W - 1 - d:CONV_W - d] * shifted
    y = _silu(acc)
    tail_ref[...] = x[t - (CONV_W - 1):, :]

    @pl.when(c < 2)
    def _():
        o_ref[...] = _l2norm_heads(y)

    @pl.when(c >= 2)
    def _():
        o_ref[...] = y


def _conv_prompt(x, w, b, batch, t):
    tc = HEAD_W
    return pl.pallas_call(
        _conv_kernel,
        grid=(batch, CONV_CH // tc),
        in_specs=[
            pl.BlockSpec((t, tc), lambda bi, c: (bi, c)),
            pl.BlockSpec((CONV_W, tc), lambda bi, c: (0, c)),
            pl.BlockSpec((1, tc), lambda bi, c: (0, c)),
        ],
        out_specs=[
            pl.BlockSpec((t, tc), lambda bi, c: (bi, c)),
            pl.BlockSpec((None, CONV_W - 1, tc), lambda bi, c: (bi, 0, c)),
        ],
        out_shape=[
            jax.ShapeDtypeStruct((batch * t, CONV_CH), F32),
            jax.ShapeDtypeStruct((batch, CONV_W - 1, CONV_CH), F32),
        ],
        compiler_params=_cparams("parallel", "parallel"),
        name="conv_prompt",
    )(x, w, b.reshape(1, CONV_CH))


def _conv_step_kernel(x_ref, buf_ref, w_ref, b_ref, o_ref, nb_ref):
    x = x_ref[...]
    w = w_ref[...]
    acc = b_ref[...] + w[CONV_W - 1:CONV_W] * x
    for j in range(CONV_W - 1):
        acc = acc + w[j:j + 1] * buf_ref[j]
    y = _silu(acc)
    qk = _l2norm_heads(y[:, :2 * HEAD_W])
    o_ref[...] = jnp.concatenate([qk, y[:, 2 * HEAD_W:]], axis=1)
    for j in range(CONV_W - 2):
        nb_ref[j] = buf_ref[j + 1]
    nb_ref[CONV_W - 2] = x


def _conv_step(x, buf, w, b):
    rows = x.shape[0]
    return pl.pallas_call(
        _conv_step_kernel,
        out_shape=[
            jax.ShapeDtypeStruct((rows, CONV_CH), F32),
            jax.ShapeDtypeStruct((CONV_W - 1, rows, CONV_CH), F32),
        ],
        compiler_params=pltpu.CompilerParams(vmem_limit_bytes=V7X_VMEM_LIMIT),
        name="conv_step",
    )(x, buf, w, b.reshape(1, CONV_CH))


def _chunk_iotas():
    r = lax.broadcasted_iota(jnp.int32, (CHUNK, CHUNK), 0)
    c = lax.broadcasted_iota(jnp.int32, (CHUNK, CHUNK), 1)
    return r, c


def _chunk_cumsum(col, r, c):
    row = jnp.sum(jnp.where(r == c, col, 0.0), axis=0, keepdims=True)
    cs_col = jnp.sum(jnp.where(c <= r, row, 0.0), axis=1, keepdims=True)
    cs_row = jnp.sum(jnp.where(r <= c, col, 0.0), axis=0, keepdims=True)
    return cs_col, cs_row


def _unit_lower_inverse(low, r, c):
    inv = (r == c).astype(F32)
    for j in range(CHUNK - 1):
        inv = inv - low[:, j:j + 1] * inv[j:j + 1, :]
    return inv


def _bf(x):
    return x.astype(BF16)


def _gdn_kernel(qkv_ref, z_ref, sm_ref, alog_ref, bias_ref, gain_ref, s0_ref, o_ref, s_ref, *, n_valid):
    t = qkv_ref.shape[0]
    s_ref[...] = s0_ref[...]
    r, c = _chunk_iotas()
    incl = r >= c
    strict = r > c
    neg_a = -jnp.exp(alog_ref[...])

    def body(n, carry):
        r0 = pl.multiple_of(n * CHUNK, CHUNK)
        rows = pl.ds(r0, CHUNK)
        sm = sm_ref[rows, :]
        beta_all = _sigmoid(sm)
        g_all = neg_a * _softplus(sm + bias_ref[...])
        if n_valid < t:
            valid = (lax.broadcasted_iota(jnp.int32, (CHUNK, 1), 0) + r0) < n_valid
            beta_all = jnp.where(valid, beta_all, 0.0)
            g_all = jnp.where(valid, g_all, 0.0)
        for h in range(GDN_HEADS):
            hs = slice(h * GDN_HD, (h + 1) * GDN_HD)
            q = qkv_ref[rows, hs] * (GDN_HD ** -0.5)
            k = qkv_ref[rows, HEAD_W + h * GDN_HD:HEAD_W + (h + 1) * GDN_HD]
            v = qkv_ref[rows, 2 * HEAD_W + h * GDN_HD:2 * HEAD_W + (h + 1) * GDN_HD]
            beta = beta_all[:, h:h + 1]
            g = g_all[:, GDN_HEADS + h:GDN_HEADS + h + 1]
            gc_col, gc_row = _chunk_cumsum(g, r, c)
            gc_last = gc_col[CHUNK - 1:CHUNK, :]
            decay = jnp.exp(jnp.where(incl, gc_col - gc_row, -jnp.inf))
            kb = k * beta
            k16 = _bf(k)
            low = jnp.where(strict, _dot_nt(_bf(kb), k16) * decay, 0.0)
            inv = _unit_lower_inverse(low, r, c)
            egc = jnp.exp(gc_col)
            rhs = jnp.concatenate([v * beta, kb * egc], axis=1)
            sol = _dot(_bf(inv), _bf(rhs))
            u = sol[:, :GDN_HD]
            w = sol[:, GDN_HD:]
            a_intra = _dot_nt(_bf(q), k16) * decay
            s = s_ref[h]
            s16 = _bf(s)
            v_new = u - _dot(_bf(w), s16)
            v16 = _bf(v_new)
            o = _dot(_bf(q * egc), s16) + _dot(_bf(a_intra), v16)
            k_dec = k * jnp.exp(gc_last - gc_col)
            s_ref[h] = s * jnp.exp(gc_last) + _dot_tn(_bf(k_dec), v16)
            z = z_ref[rows, hs]
            o_ref[rows, hs] = (_rms(o, gain_ref[...]) * _silu(z)).astype(o_ref.dtype)
        return carry

    lax.fori_loop(0, t // CHUNK, body, 0)


def _gdn(conv_out, z, small, a_log, dt_bias, gain, s0, batch, t, n_valid):
    pad = jnp.zeros((LANES - 2 * GDN_HEADS,), F32)
    alog_vec = jnp.concatenate([jnp.zeros((GDN_HEADS,), F32), a_log, pad]).reshape(1, LANES)
    bias_vec = jnp.concatenate([jnp.zeros((GDN_HEADS,), F32), dt_bias, pad]).reshape(1, LANES)
    kern = functools.partial(_gdn_kernel, n_valid=n_valid)
    vec = pl.BlockSpec((1, LANES), lambda b: (0, 0))
    st = pl.BlockSpec((None, GDN_HEADS, GDN_HD, GDN_HD), lambda b: (b, 0, 0, 0))
    return pl.pallas_call(
        kern,
        grid=(batch,),
        in_specs=[
            pl.BlockSpec((t, 3 * HEAD_W), lambda b: (b, 0)),
            pl.BlockSpec((t, HEAD_W), lambda b: (b, 0)),
            pl.BlockSpec((t, LANES), lambda b: (b, 0)),
            vec, vec, vec, st,
        ],
        out_specs=[pl.BlockSpec((t, HEAD_W), lambda b: (b, 0)), st],
        out_shape=[
            jax.ShapeDtypeStruct((batch * t, HEAD_W), BF16),
            jax.ShapeDtypeStruct((batch, GDN_HEADS, GDN_HD, GDN_HD), F32),
        ],
        compiler_params=_cparams("parallel"),
        name="gdn",
    )(conv_out, z, small, alog_vec, bias_vec, gain.reshape(1, LANES), s0)


_DT_LANE = 2 * GDN_HEADS


def _ssd_kernel(x_ref, bc_ref, z_ref, sm_ref, alog_ref, bias_ref, d_ref, gain_ref, h0_ref,
                y_ref, h_ref, *, n_valid):
    t = x_ref.shape[0]
    h_ref[...] = h0_ref[...]
    r, c = _chunk_iotas()
    incl = r >= c
    neg_a = -jnp.exp(alog_ref[...])
    lane = lax.broadcasted_iota(jnp.int32, (CHUNK, LANES), 1)
    first = lane < SSM_HD
    srow = lax.broadcasted_iota(jnp.int32, (LANES, 1), 0) < SSM_HD
    pairs_per_group = SSM_HEADS // SSM_GROUPS // 2

    def body(n, carry):
        r0 = pl.multiple_of(n * CHUNK, CHUNK)
        rows = pl.ds(r0, CHUNK)
        dt_all = _softplus(sm_ref[rows, :] + bias_ref[...])
        if n_valid < t:
            valid = (lax.broadcasted_iota(jnp.int32, (CHUNK, 1), 0) + r0) < n_valid
            dt_all = jnp.where(valid, dt_all, 0.0)
        dta_all = dt_all * neg_a
        for g in range(SSM_GROUPS):
            bg = _bf(bc_ref[rows, g * SSM_STATE:(g + 1) * SSM_STATE])
            cg = _bf(bc_ref[rows, (SSM_GROUPS + g) * SSM_STATE:(SSM_GROUPS + g + 1) * SSM_STATE])
            cb = _dot_nt(cg, bg)
            ys = []
            for pr in range(pairs_per_group):
                p = g * pairs_per_group + pr
                ps = slice(p * LANES, (p + 1) * LANES)
                xp = x_ref[rows, ps]
                cols = []
                for hh in (2 * p, 2 * p + 1):
                    dt = dt_all[:, _DT_LANE + hh:_DT_LANE + hh + 1]
                    ac_col, ac_row = _chunk_cumsum(dta_all[:, _DT_LANE + hh:_DT_LANE + hh + 1], r, c)
                    lmask = jnp.exp(jnp.where(incl, ac_col - ac_row, -jnp.inf))
                    cols.append((dt, ac_col, lmask))
                (dt_a, ac_a, lm_a), (dt_b, ac_b, lm_b) = cols
                last_a = ac_a[CHUNK - 1:CHUNK, :]
                last_b = ac_b[CHUNK - 1:CHUNK, :]
                xdt = xp * jnp.where(first, dt_a, dt_b)
                y_intra = (_dot(_bf(cb * lm_a), _bf(jnp.where(first, xdt, 0.0)))
                           + _dot(_bf(cb * lm_b), _bf(jnp.where(first, 0.0, xdt))))
                hp = h_ref[p]
                y_inter = _dot_nt(cg, _bf(hp)) * jnp.where(first, jnp.exp(ac_a), jnp.exp(ac_b))
                x_dec = xdt * jnp.where(first, jnp.exp(last_a - ac_a), jnp.exp(last_b - ac_b))
                h_ref[p] = hp * jnp.where(srow, jnp.exp(last_a), jnp.exp(last_b)) + _dot_tn(_bf(x_dec), bg)
                y = y_intra + y_inter + d_ref[:, ps] * xp
                ys.append(y * _silu(z_ref[rows, ps]))
            ms = sum(jnp.sum(y * y, axis=-1, keepdims=True) for y in ys) / (len(ys) * LANES)
            scale = lax.rsqrt(ms + NORM_EPS)
            for pr, y in enumerate(ys):
                ps = slice((g * pairs_per_group + pr) * LANES, (g * pairs_per_group + pr + 1) * LANES)
                y_ref[rows, ps] = (y * scale * gain_ref[:, ps]).astype(y_ref.dtype)
        return carry

    lax.fori_loop(0, t // CHUNK, body, 0)


def _ssd(conv_out, z, small, a_log, dt_bias, d_skip, gain, h0, batch, t, n_valid):
    npair = SSM_HEADS // 2
    pad_l = jnp.zeros((_DT_LANE,), F32)
    pad_r = jnp.zeros((LANES - _DT_LANE - SSM_HEADS,), F32)
    alog_vec = jnp.concatenate([pad_l, a_log, pad_r]).reshape(1, LANES)
    bias_vec = jnp.concatenate([pad_l, dt_bias, pad_r]).reshape(1, LANES)
    d_vec = jnp.repeat(d_skip, SSM_HD).reshape(1, HEAD_W)
    kern = functools.partial(_ssd_kernel, n_valid=n_valid)
    vec = pl.BlockSpec((1, LANES), lambda b: (0, 0))
    wide = pl.BlockSpec((1, HEAD_W), lambda b: (0, 0))
    st = pl.BlockSpec((None, npair, LANES, SSM_STATE), lambda b: (b, 0, 0, 0))
    y, h = pl.pallas_call(
        kern,
        grid=(batch,),
        in_specs=[
            pl.BlockSpec((t, HEAD_W), lambda b: (b, 3)),
            pl.BlockSpec((t, HEAD_W), lambda b: (b, 4)),
            pl.BlockSpec((t, HEAD_W), lambda b: (b, 1)),
            pl.BlockSpec((t, LANES), lambda b: (b, 0)),
            vec, vec, wide, wide, st,
        ],
        out_specs=[pl.BlockSpec((t, HEAD_W), lambda b: (b, 0)), st],
        out_shape=[
            jax.ShapeDtypeStruct((batch * t, HEAD_W), BF16),
            jax.ShapeDtypeStruct((batch, npair, LANES, SSM_STATE), F32),
        ],
        compiler_params=_cparams("parallel"),
        name="ssd",
    )(conv_out, conv_out, z, small, alog_vec, bias_vec, d_vec, gain.reshape(1, HEAD_W),
      h0.reshape(batch, npair, LANES, SSM_STATE))
    return y, h.reshape(batch, SSM_HEADS, SSM_HD, SSM_STATE)


_ROWS = 8


def _moba_step_kernel(pt_ref, qkv_ref, page_ref, o_ref, m_ref, l_ref, acc_ref, ks_ref, *, n_sel):
    j = pl.program_id(1)
    npg = pl.num_programs(1)
    scale = MOBA_HD ** -0.5
    page = page_ref[...]
    for h in range(MOBA_HEADS):
        hs = slice(h * MOBA_HD, (h + 1) * MOBA_HD)
        q8 = _bf(jnp.broadcast_to(qkv_ref[:, hs], (_ROWS, MOBA_HD)))
        kp = page[:, hs]
        vp = page[:, HEAD_W + h * MOBA_HD:HEAD_W + (h + 1) * MOBA_HD]
        s = _dot_nt(q8, _bf(kp))[0:1] * scale
        m = jnp.max(s, axis=-1, keepdims=True)
        e = jnp.exp(s - m)
        m_ref[h, pl.ds(j, 1), :] = jnp.broadcast_to(m, (1, LANES))
        l_ref[h, pl.ds(j, 1), :] = jnp.broadcast_to(jnp.sum(e, axis=-1, keepdims=True), (1, LANES))
        acc_ref[h, pl.ds(j, 1), :] = _dot(_bf(jnp.broadcast_to(e, (_ROWS, PAGE_SIZE))), _bf(vp))[0:1]
        ks_ref[h, pl.ds(j, 1), :] = jnp.sum(kp, axis=0, keepdims=True)

    @pl.when(j == npg - 1)
    def _():
        n = m_ref.shape[1]
        r = lax.broadcasted_iota(jnp.int32, (n, n), 0)
        c = lax.broadcasted_iota(jnp.int32, (n, n), 1)
        prow = lax.broadcasted_iota(jnp.int32, (n, 1), 0)
        outs = []
        for h in range(MOBA_HEADS):
            hs = slice(h * MOBA_HD, (h + 1) * MOBA_HD)
            q = qkv_ref[:, hs]
            k_new = qkv_ref[:, HEAD_W + h * MOBA_HD:HEAD_W + (h + 1) * MOBA_HD]
            v_new = qkv_ref[:, 2 * HEAD_W + h * MOBA_HD:2 * HEAD_W + (h + 1) * MOBA_HD]
            gp = jnp.sum(ks_ref[h] * q, axis=-1, keepdims=True) * (1.0 / MOBA_BLOCK)
            nxt = pltpu.roll(gp, n - 1, 0)
            prv = pltpu.roll(gp, 1, 0)
            gb = gp + jnp.where(prow % 2 == 0, nxt, prv)
            gb_row = jnp.sum(jnp.where(r == c, gb, 0.0), axis=0, keepdims=True)
            beats = (c % 2 == 0) & ((gb_row > gb) | ((gb_row == gb) & (c // 2 < r // 2)))
            rank = jnp.sum(beats.astype(jnp.int32), axis=1, keepdims=True)
            sel = rank < n_sel
            s_own = jnp.sum(_bf(q).astype(F32) * _bf(k_new).astype(F32), axis=-1, keepdims=True) * scale
            mcol = m_ref[h][:, 0:1]
            m_tot = jnp.maximum(jnp.max(jnp.where(sel, mcol, -jnp.inf), axis=0, keepdims=True), s_own)
            w = jnp.where(sel, jnp.exp(mcol - m_tot), 0.0)
            e_own = jnp.exp(s_own - m_tot)
            l_tot = jnp.sum(w * l_ref[h][:, 0:1], axis=0, keepdims=True) + e_own
            acc = jnp.sum(w * acc_ref[h], axis=0, keepdims=True) + e_own * _bf(v_new).astype(F32)
            outs.append(acc / l_tot)
        o_ref[...] = jnp.concatenate(outs, axis=1).astype(o_ref.dtype)


def _moba_step(qkv, cache, page_table, layer):
    rows, npg = page_table.shape
    n_blocks = npg * PAGE_SIZE // MOBA_BLOCK + 1
    kern = functools.partial(_moba_step_kernel, n_sel=min(MOBA_TOPK, n_blocks))
    stat = pltpu.VMEM((MOBA_HEADS, npg, LANES), F32)
    return pl.pallas_call(
        kern,
        grid_spec=pltpu.PrefetchScalarGridSpec(
            num_scalar_prefetch=1,
            grid=(rows, npg),
            in_specs=[
                pl.BlockSpec((None, 1, 3 * HEAD_W), lambda b, j, pt: (b, 0, 0)),
                pl.BlockSpec((None, None, PAGE_SIZE, 2 * HEAD_W), lambda b, j, pt: (pt[b, j], layer, 0, 0)),
            ],
            out_specs=pl.BlockSpec((None, 1, HEAD_W), lambda b, j, pt: (b, 0, 0)),
            scratch_shapes=[stat, stat, stat, stat],
        ),
        out_shape=jax.ShapeDtypeStruct((rows, 1, HEAD_W), BF16),
        compiler_params=_cparams("parallel", "arbitrary"),
        name="moba_step",
    )(page_table, qkv, cache)


def _diff_step_kernel(pt_ref, qkv_ref, page_ref, lp_ref, g_ref, o_ref, m_ref, l_ref, acc_ref, *, lam_init):
    j = pl.program_id(1)
    npg = pl.num_programs(1)
    scale = DIFF_HD ** -0.5
    c0 = _C_DIFF

    @pl.when(j == 0)
    def _():
        m_ref[...] = jnp.full(m_ref.shape, -jnp.inf, F32)
        l_ref[...] = jnp.zeros_like(l_ref)
        acc_ref[...] = jnp.zeros_like(acc_ref)

    lane = lax.broadcasted_iota(jnp.int32, (_ROWS, LANES), 1)
    sub = lax.broadcasted_iota(jnp.int32, (_ROWS, LANES), 0)
    comp = ((sub == 0) & (lane < DIFF_HD)) | ((sub == 1) & (lane >= DIFF_HD))
    page = page_ref[...]

    def q_tile(h):
        q = jnp.broadcast_to(qkv_ref[:, c0 + h * LANES:c0 + (h + 1) * LANES], (_ROWS, LANES))
        return _bf(jnp.where(comp, q, 0.0))

    def update(h, s, v16):
        m_old = m_ref[h][:, 0:1]
        m_new = jnp.maximum(m_old, jnp.max(s, axis=-1, keepdims=True))
        alpha = jnp.exp(m_old - m_new)
        e = jnp.exp(s - m_new)
        l_new = alpha * l_ref[h][:, 0:1] + jnp.sum(e, axis=-1, keepdims=True)
        acc_new = alpha * acc_ref[h] + _dot(_bf(e), v16)
        m_ref[h] = jnp.broadcast_to(m_new, (_ROWS, LANES))
        l_ref[h] = jnp.broadcast_to(l_new, (_ROWS, LANES))
        acc_ref[h] = acc_new
        return l_new, acc_new

    for h in range(DIFF_HEADS):
        kp = page[:, h * LANES:(h + 1) * LANES]
        vp = page[:, HEAD_W + h * LANES:HEAD_W + (h + 1) * LANES]
        update(h, _dot_nt(q_tile(h), _bf(kp)) * scale, _bf(vp))

    @pl.when(j == npg - 1)
    def _():
        lam = _diff_lambda(lp_ref, lam_init)
        outs = []
        for h in range(DIFF_HEADS):
            k_new = qkv_ref[:, c0 + HEAD_W + h * LANES:c0 + HEAD_W + (h + 1) * LANES]
            v_new = qkv_ref[:, c0 + 2 * HEAD_W + h * LANES:c0 + 2 * HEAD_W + (h + 1) * LANES]
            s_new = jnp.sum(q_tile(h).astype(F32) * _bf(k_new).astype(F32), axis=-1, keepdims=True) * scale
            m_old = m_ref[h][:, 0:1]
            m_new = jnp.maximum(m_old, s_new)
            alpha = jnp.exp(m_old - m_new)
            e = jnp.exp(s_new - m_new)
            l_new = alpha * l_ref[h][:, 0:1] + e
            acc = alpha * acc_ref[h] + e * _bf(v_new).astype(F32)
            o = acc / l_new
            d = o[0:1] - lam * o[1:2]
            outs.append(_rms(d, g_ref[...]) * (1.0 - lam_init))
        o_ref[...] = jnp.concatenate(outs, axis=1).astype(o_ref.dtype)


def _diff_step(qkv, cache, page_table, layer, lam_params, norm_gain, lam_init):
    rows, npg = page_table.shape
    kern = functools.partial(_diff_step_kernel, lam_init=lam_init)
    stat = pltpu.VMEM((DIFF_HEADS, _ROWS, LANES), F32)
    return pl.pallas_call(
        kern,
        grid_spec=pltpu.PrefetchScalarGridSpec(
            num_scalar_prefetch=1,
            grid=(rows, npg),
            in_specs=[
                pl.BlockSpec((None, 1, 6 * HEAD_W), lambda b, j, pt: (b, 0, 0)),
                pl.BlockSpec((None, None, PAGE_SIZE, 2 * HEAD_W), lambda b, j, pt: (pt[b, j], layer, 0, 0)),
                pl.BlockSpec((4, DIFF_HD), lambda b, j, pt: (0, 0)),
                pl.BlockSpec((1, LANES), lambda b, j, pt: (0, 0)),
            ],
            out_specs=pl.BlockSpec((None, 1, HEAD_W), lambda b, j, pt: (b, 0, 0)),
            scratch_shapes=[stat, stat, stat],
        ),
        out_shape=jax.ShapeDtypeStruct((rows, 1, HEAD_W), BF16),
        compiler_params=_cparams("parallel", "arbitrary"),
        name="diff_step",
    )(page_table, qkv, cache, lam_params, norm_gain.reshape(1, LANES))


def _rope_tables(pos, hd, reps):
    rd = hd // 4
    half = rd // 2
    inv_freq = 1.0 / (ROPE_THETA ** (jnp.arange(half, dtype=F32) * (2.0 / rd)))
    ang = pos.astype(F32)[:, None] * inv_freq[None, :]
    cos = jnp.cos(ang)
    sin = jnp.sin(ang)
    n = pos.shape[0]
    zero = jnp.zeros((n, half), F32)
    c = jnp.concatenate([cos, cos, jnp.ones((n, hd - rd), F32)], axis=1)
    s1 = jnp.concatenate([zero, sin, jnp.zeros((n, hd - rd), F32)], axis=1)
    s2 = jnp.concatenate([-sin, zero, jnp.zeros((n, hd - rd), F32)], axis=1)
    return tuple(jnp.tile(a, (1, reps)) for a in (c, s1, s2))


def _pad_chunk(a):
    rows, n = a.shape
    return jnp.pad(a[:, None, :], ((0, 0), (0, CHUNK - 1), (0, 0))).reshape(rows * CHUNK, n)


def kernel(x_prompt, x_sample, cache_moba_kv, cache_diff_kv, state_gdn, state_ssm, state_conv, page_table, ffn1_norm, ffn1_w_gu, ffn1_w_down, mix_norm, w_in, conv_w, conv_b, diff_lambda, diff_norm, gdn_A_log, gdn_dt_bias, gdn_norm, ssm_A_log, ssm_dt_bias, ssm_D, ssm_norm, w_branch, w_out, ffn2_norm, ffn2_w_gu, ffn2_w_down, final_norm):
    bp, t, d = x_prompt.shape
    bs = x_sample.shape[0]
    depth = w_in.shape[0]
    assert d == D_MODEL and x_sample.shape[1] == 1 and t % MOBA_BLOCK == 0
    past_len = page_table.shape[1] * PAGE_SIZE
    assert past_len % MOBA_BLOCK == 0
    n_phys = cache_moba_kv.shape[0]
    cache_m = cache_moba_kv.reshape(n_phys, depth, PAGE_SIZE, 2 * HEAD_W)
    cache_d = cache_diff_kv.reshape(n_phys, depth, PAGE_SIZE, 2 * HEAD_W)

    tm_p = 512 if (bp * t) % 512 == 0 else MOBA_BLOCK
    pos_p = jnp.arange(t, dtype=jnp.int32)
    pos_s = jnp.full((bs,), past_len, jnp.int32)
    tables_p = _rope_tables(pos_p, MOBA_HD, 1) + _rope_tables(pos_p, DIFF_HD, 2)
    tables_s = _rope_tables(pos_s, MOBA_HD, 1) + _rope_tables(pos_s, DIFF_HD, 2)

    hp = x_prompt.reshape(bp * t, d)
    hs = x_sample.reshape(bs, d)
    outs = {k: [] for k in ("moba_p", "moba_s", "diff_p", "diff_s", "gdn_p", "gdn_s", "ssm_p", "ssm_s",
                            "conv_p", "conv_s")}
    y_p = y_s = None
    for l in range(depth):
        last = l == depth - 1
        lam_init = 0.8 - 0.6 * math.exp(-0.3 * l)
        wi = w_in[l]
        w_qkv = wi[:, _C_MOBA:_C_CONV].astype(BF16)
        w_conv = wi[:, _C_CONV:_C_GDNZ].astype(BF16)
        w_z = jnp.concatenate([wi[:, _C_GDNZ:_C_GDNB], wi[:, _C_SSMZ:_C_SSMDT]], axis=1).astype(BF16)
        w_small = jnp.concatenate([wi[:, _C_GDNB:_C_SSMZ], wi[:, _C_SSMDT:_C_GATE],
                                   jnp.zeros((d, LANES - 4 * GDN_HEADS), F32)], axis=1).astype(BF16)
        w_gate = jnp.transpose(wi[:, _C_GATE:].reshape(d, N_BRANCH, d), (1, 0, 2)).astype(BF16)
        w_br = w_branch[l].astype(BF16)
        w_o = w_out[l].astype(BF16)
        wgu1 = ffn1_w_gu[l].astype(BF16)
        wd1 = ffn1_w_down[l].astype(BF16)
        wgu2 = ffn2_w_gu[l].astype(BF16)
        wd2 = ffn2_w_down[l].astype(BF16)
        post_gain = final_norm if last else ffn1_norm[l + 1]
        post_dtype = F32 if last else BF16

        def dense_in(h, tm, tables, rows_per_seq):
            h1, u = _ffn(h, ffn1_norm[l], wgu1, wd1, mix_norm[l], BF16, tm)
            qkv = _qkv_proj(u, w_qkv, tables, tm, rows_per_seq)
            conv_in = _matmul(u, w_conv, tm, HEAD_W)
            z = _matmul(u, w_z, tm, HEAD_W)
            small = _matmul(u, w_small, tm, LANES)
            return h1, u, qkv, conv_in, z, small

        def dense_out(h1, u, branches, tm):
            merged = _merge(u, branches, w_gate, w_br, tm)
            h2 = _outproj(merged, w_o, h1, tm)
            return _ffn(h2, ffn2_norm[l], wgu2, wd2, post_gain, post_dtype, tm)

        h1, u, qkv, conv_in, z, small = dense_in(hp, tm_p, tables_p, t)
        o_moba = _moba_prompt(qkv, bp, t)
        o_diff = _diff_prompt(qkv, diff_lambda[l], diff_norm[l], lam_init, bp, t)
        conv_out, conv_new = _conv_prompt(conv_in, conv_w[l], conv_b[l], bp, t)
        o_gdn, s_new = _gdn(conv_out, z, small, gdn_A_log[l], gdn_dt_bias[l], gdn_norm[l],
                            jnp.zeros((bp, GDN_HEADS, GDN_HD, GDN_HD), F32), bp, t, t)
        y_ssm, h_new = _ssd(conv_out, z, small, ssm_A_log[l], ssm_dt_bias[l], ssm_D[l], ssm_norm[l],
                            jnp.zeros((bp, SSM_HEADS, SSM_HD, SSM_STATE), F32), bp, t, t)
        hp, y_p = dense_out(h1, u, (o_moba, o_diff, o_gdn, y_ssm), tm_p)
        outs["moba_p"].append(qkv[:, HEAD_W:3 * HEAD_W].reshape(bp, t, 2, MOBA_HEADS, MOBA_HD))
        outs["diff_p"].append(qkv[:, _C_DIFF + HEAD_W:].reshape(bp, t, 2, DIFF_HEADS, 2 * DIFF_HD))
        outs["gdn_p"].append(s_new)
        outs["ssm_p"].append(h_new)
        outs["conv_p"].append(conv_new)

        h1, u, qkv, conv_in, z, small = dense_in(hs, bs, tables_s, 1)
        qkv3 = qkv.reshape(bs, 1, 6 * HEAD_W)
        o_moba = _moba_step(qkv3, cache_m, page_table, l).reshape(bs, HEAD_W)
        o_diff = _diff_step(qkv3, cache_d, page_table, l, diff_lambda[l], diff_norm[l], lam_init).reshape(bs, HEAD_W)
        conv_out, buf_new = _conv_step(conv_in, jnp.moveaxis(state_conv[l], 1, 0), conv_w[l], conv_b[l])
        conv_pad, z_pad, small_pad = _pad_chunk(conv_out), _pad_chunk(z), _pad_chunk(small)
        o_gdn, s_new = _gdn(conv_pad, z_pad, small_pad, gdn_A_log[l], gdn_dt_bias[l], gdn_norm[l],
                            state_gdn[l], bs, CHUNK, 1)
        y_ssm, h_new = _ssd(conv_pad, z_pad, small_pad, ssm_A_log[l], ssm_dt_bias[l], ssm_D[l], ssm_norm[l],
                            state_ssm[l], bs, CHUNK, 1)
        o_gdn = o_gdn.reshape(bs, CHUNK, HEAD_W)[:, 0]
        y_ssm = y_ssm.reshape(bs, CHUNK, HEAD_W)[:, 0]
        hs, y_s = dense_out(h1, u, (o_moba, o_diff, o_gdn, y_ssm), bs)
        outs["moba_s"].append(qkv[:, HEAD_W:3 * HEAD_W].reshape(bs, 1, 2, MOBA_HEADS, MOBA_HD))
        outs["diff_s"].append(qkv[:, _C_DIFF + HEAD_W:].reshape(bs, 1, 2, DIFF_HEADS, 2 * DIFF_HD))
        outs["gdn_s"].append(s_new)
        outs["ssm_s"].append(h_new)
        outs["conv_s"].append(jnp.moveaxis(buf_new, 0, 1))

    return (y_p.reshape(bp, t, d), y_s.reshape(bs, 1, d),
            jnp.stack(outs["moba_p"], axis=1), jnp.stack(outs["moba_s"], axis=1),
            jnp.stack(outs["diff_p"], axis=1), jnp.stack(outs["diff_s"], axis=1),
            jnp.stack(outs["gdn_p"]), jnp.stack(outs["gdn_s"]),
            jnp.stack(outs["ssm_p"]), jnp.stack(outs["ssm_s"]),
            jnp.stack(outs["conv_p"]), jnp.stack(outs["conv_s"]))
```

```python
import functools
import math

import jax
import jax.numpy as jnp
from jax import lax
from jax.experimental import pallas as pl
from jax.experimental.pallas import tpu as pltpu

F32 = jnp.float32
BF16 = jnp.bfloat16

D_MODEL = 2048
D_FF = 5632
NORM_EPS = 1e-6
ROPE_THETA = 500000.0
PAGE_SIZE = 128

MOBA_HEADS = 4
MOBA_HD = 128
MOBA_BLOCK = 256
MOBA_TOPK = 3
DIFF_HEADS = 4
DIFF_HD = 64
GDN_HEADS = 4
GDN_HD = 128
SSM_HEADS = 8
SSM_HD = 64
SSM_GROUPS = 2
SSM_STATE = 128
CHUNK = 64
CONV_W = 4
HEAD_W = 512
CONV_CH = 2560
N_BRANCH = 4

LANES = 128
V7X_VMEM_LIMIT = 56 * 1024 * 1024

_C_MOBA = 0
_C_DIFF = 1536
_C_CONV = 3072
_C_GDNZ = 5632
_C_GDNB = 6144
_C_GDNA = 6148
_C_SSMZ = 6152
_C_SSMDT = 6664
_C_GATE = 6672


def _cparams(*sem):
    return pltpu.CompilerParams(dimension_semantics=sem, vmem_limit_bytes=V7X_VMEM_LIMIT)


def _rms(x, gain):
    y = x * lax.rsqrt(jnp.mean(x * x, axis=-1, keepdims=True) + NORM_EPS)
    return y * gain


def _dot(a, b):
    return jnp.dot(a, b, preferred_element_type=F32)


def _dot_nt(a, b):
    return lax.dot_general(a, b, (((1,), (1,)), ((), ())), preferred_element_type=F32)


def _dot_tn(a, b):
    return lax.dot_general(a, b, (((0,), (0,)), ((), ())), preferred_element_type=F32)


def _softplus(x):
    return jnp.maximum(x, 0.0) + jnp.log(1.0 + jnp.exp(-jnp.abs(x)))


def _sigmoid(x):
    return 1.0 / (1.0 + jnp.exp(-x))


def _silu(x):
    return x * _sigmoid(x)


_FFN_SLICES = 2


def _ffn_kernel(x_ref, g_ref, wg_ref, wu_ref, wd_ref, pg_ref, o_ref, p_ref, xn_ref):
    f = pl.program_id(1)

    @pl.when(f == 0)
    def _():
        x = x_ref[...]
        xn_ref[...] = _rms(x, g_ref[...]).astype(BF16)
        o_ref[...] = x

    xn = xn_ref[...]
    tf = wg_ref.shape[1]
    w = tf // _FFN_SLICES
    gu = [(_dot(xn, wg_ref[:, c * w:(c + 1) * w]), _dot(xn, wu_ref[:, c * w:(c + 1) * w]))
          for c in range(_FFN_SLICES)]
    acc = None
    for c, (gate, up) in enumerate(gu):
        act = (0.5 * _silu(gate) * up).astype(BF16)
        part = _dot(act, wd_ref[c * w:(c + 1) * w, :])
        acc = part if acc is None else acc + part
    o_ref[...] += acc

    @pl.when(f == pl.num_programs(1) - 1)
    def _():
        p_ref[...] = _rms(o_ref[...], pg_ref[...]).astype(p_ref.dtype)


def _ffn(x, gain, w_gu, w_down, post_gain, post_dtype, tm, tf=512):
    m = x.shape[0]
    nf = D_FF // tf
    return pl.pallas_call(
        _ffn_kernel,
        grid=(m // tm, nf),
        in_specs=[
            pl.BlockSpec((tm, D_MODEL), lambda i, f: (i, 0)),
            pl.BlockSpec((1, D_MODEL), lambda i, f: (0, 0)),
            pl.BlockSpec((D_MODEL, tf), lambda i, f: (0, f)),
            pl.BlockSpec((D_MODEL, tf), lambda i, f: (0, f + nf)),
            pl.BlockSpec((tf, D_MODEL), lambda i, f: (f, 0)),
            pl.BlockSpec((1, D_MODEL), lambda i, f: (0, 0)),
        ],
        out_specs=[
            pl.BlockSpec((tm, D_MODEL), lambda i, f: (i, 0)),
            pl.BlockSpec((tm, D_MODEL), lambda i, f: (i, 0)),
        ],
        out_shape=[
            jax.ShapeDtypeStruct((m, D_MODEL), F32),
            jax.ShapeDtypeStruct((m, D_MODEL), post_dtype),
        ],
        scratch_shapes=[pltpu.VMEM((tm, D_MODEL), BF16)],
        compiler_params=_cparams("parallel", "arbitrary"),
        name="ffn",
    )(x, gain.reshape(1, D_MODEL), w_gu, w_gu, w_down, post_gain.reshape(1, D_MODEL))


def _rope(x, c_ref, s1_ref, s2_ref, half):
    n = x.shape[1]
    reps = n // LANES
    c = jnp.concatenate([c_ref[...]] * reps, axis=1)
    s1 = jnp.concatenate([s1_ref[...]] * reps, axis=1)
    s2 = jnp.concatenate([s2_ref[...]] * reps, axis=1)
    return x * c + pltpu.roll(x, half, 1) * s1 + pltpu.roll(x, n - half, 1) * s2


def _qkv_kernel(u_ref, w_ref, mc_ref, ms1_ref, ms2_ref, dc_ref, ds1_ref, ds2_ref, o_ref):
    j = pl.program_id(1)
    acc = _dot(u_ref[...], w_ref[...])

    @pl.when(j < 2)
    def _():
        o_ref[...] = _rope(acc, mc_ref, ms1_ref, ms2_ref, MOBA_HD // 8)

    @pl.when((j == 3) | (j == 4))
    def _():
        o_ref[...] = _rope(acc, dc_ref, ds1_ref, ds2_ref, DIFF_HD // 8)

    @pl.when((j == 2) | (j == 5))
    def _():
        o_ref[...] = acc


def _qkv_proj(u, w, tables, tm, rows_per_seq):
    m = u.shape[0]
    tn = HEAD_W
    nt = rows_per_seq // tm if rows_per_seq >= tm else 1
    tspec = pl.BlockSpec((tm, LANES), lambda i, j: (i % nt, 0))
    return pl.pallas_call(
        _qkv_kernel,
        grid=(m // tm, w.shape[1] // tn),
        in_specs=[
            pl.BlockSpec((tm, D_MODEL), lambda i, j: (i, 0)),
            pl.BlockSpec((D_MODEL, tn), lambda i, j: (0, j)),
        ] + [tspec] * 6,
        out_specs=pl.BlockSpec((tm, tn), lambda i, j: (i, j)),
        out_shape=jax.ShapeDtypeStruct((m, w.shape[1]), F32),
        compiler_params=_cparams("parallel", "arbitrary"),
        name="qkv_proj",
    )(u, w, *tables)


def _mm_kernel(u_ref, w_ref, o_ref):
    o_ref[...] = _dot(u_ref[...], w_ref[...]).astype(o_ref.dtype)


def _matmul(u, w, tm, tn, out_dtype=F32):
    m, k = u.shape
    n = w.shape[1]
    return pl.pallas_call(
        _mm_kernel,
        grid=(m // tm, n // tn),
        in_specs=[
            pl.BlockSpec((tm, k), lambda i, j: (i, 0)),
            pl.BlockSpec((k, tn), lambda i, j: (0, j)),
        ],
        out_specs=pl.BlockSpec((tm, tn), lambda i, j: (i, j)),
        out_shape=jax.ShapeDtypeStruct((m, n), out_dtype),
        compiler_params=_cparams("parallel", "arbitrary"),
        name="proj",
    )(u, w)


def _merge_kernel(u_ref, b0_ref, b1_ref, b2_ref, b3_ref, wg_ref, wb_ref, o_ref):
    u = u_ref[...]
    acc = None
    for i, b_ref in enumerate((b0_ref, b1_ref, b2_ref, b3_ref)):
        gate = _sigmoid(_dot(u, wg_ref[i]))
        term = gate * _dot(b_ref[...], wb_ref[i])
        acc = term if acc is None else acc + term
    o_ref[...] = acc.astype(o_ref.dtype)


def _merge(u, branches, w_gate, w_branch, tm, tn=256):
    m = u.shape[0]
    bspec = pl.BlockSpec((tm, HEAD_W), lambda i, j: (i, 0))
    return pl.pallas_call(
        _merge_kernel,
        grid=(m // tm, D_MODEL // tn),
        in_specs=[pl.BlockSpec((tm, D_MODEL), lambda i, j: (i, 0)), bspec, bspec, bspec, bspec,
                  pl.BlockSpec((N_BRANCH, D_MODEL, tn), lambda i, j: (0, 0, j)),
                  pl.BlockSpec((N_BRANCH, HEAD_W, tn), lambda i, j: (0, 0, j))],
        out_specs=pl.BlockSpec((tm, tn), lambda i, j: (i, j)),
        out_shape=jax.ShapeDtypeStruct((m, D_MODEL), BF16),
        compiler_params=_cparams("parallel", "arbitrary"),
        name="merge",
    )(u, *branches, w_gate, w_branch)


def _outproj_kernel(a_ref, w_ref, h_ref, o_ref):
    o_ref[...] = h_ref[...] + _dot(a_ref[...], w_ref[...])


def _outproj(a, w, h, tm, tn=512):
    m = a.shape[0]
    return pl.pallas_call(
        _outproj_kernel,
        grid=(m // tm, D_MODEL // tn),
        in_specs=[pl.BlockSpec((tm, D_MODEL), lambda i, j: (i, 0)),
                  pl.BlockSpec((D_MODEL, tn), lambda i, j: (0, j)),
                  pl.BlockSpec((tm, tn), lambda i, j: (i, j))],
        out_specs=pl.BlockSpec((tm, tn), lambda i, j: (i, j)),
        out_shape=jax.ShapeDtypeStruct((m, D_MODEL), F32),
        compiler_params=_cparams("parallel", "arbitrary"),
        name="outproj",
    )(a, w, h)


def _split_hi_lo(x):
    hi = x.astype(BF16)
    lo = (x - hi.astype(F32)).astype(BF16)
    return hi, lo


def _moba_kernel(q_ref, k_ref, v_ref, o_ref, kb_ref, vb_ref, km_ref, *, nb, n_sel):
    qi = pl.program_id(2)
    t = k_ref.shape[0]
    bq = q_ref.shape[0]

    @pl.when(qi == 0)
    def _():
        k = k_ref[...]
        kb_ref[...] = k.astype(BF16)
        vb_ref[...] = v_ref[...].astype(BF16)
        km_ref[...] = jnp.zeros_like(km_ref)
        for n in range(nb):
            km_ref[n:n + 1, :] = jnp.mean(k[n * MOBA_BLOCK:(n + 1) * MOBA_BLOCK, :], axis=0, keepdims=True)

    q = q_ref[...]
    qh, ql = _split_hi_lo(q)
    kh, kl = _split_hi_lo(km_ref[...])
    gate = _dot_nt(qh, kh) + (_dot_nt(qh, kl) + _dot_nt(ql, kh))
    lane = lax.broadcasted_iota(jnp.int32, gate.shape, 1)
    past = lane < qi
    gate = jnp.where(past, gate, -jnp.inf)
    rank = jnp.zeros(gate.shape, jnp.int32)
    for m in range(nb):
        gm = gate[:, m:m + 1]
        beats = (gm > gate) | ((gm == gate) & (m < lane))
        rank = rank + beats.astype(jnp.int32)
    sel = (past & (rank < n_sel)).astype(F32).astype(BF16)
    q16 = q.astype(BF16)

    def attend(tk):
        erow = lax.broadcasted_iota(jnp.int32, (LANES, tk), 0)
        ecol = lax.broadcasted_iota(jnp.int32, (LANES, tk), 1)
        expand = (ecol // MOBA_BLOCK == erow).astype(F32).astype(BF16)
        chosen = _dot(sel, expand) > 0.5
        row = lax.broadcasted_iota(jnp.int32, (bq, tk), 0) + (tk - bq)
        col = lax.broadcasted_iota(jnp.int32, (bq, tk), 1)
        own = (col >= tk - bq) & (col <= row)
        s = _dot_nt(q16, kb_ref[0:tk, :]) * (MOBA_HD ** -0.5)
        s = jnp.where(chosen | own, s, -jnp.inf)
        p = jnp.exp(s - jnp.max(s, axis=-1, keepdims=True))
        p = p / jnp.sum(p, axis=-1, keepdims=True)
        o_ref[...] = _dot(p.astype(BF16), vb_ref[0:tk, :]).astype(o_ref.dtype)

    for n in range(t // bq):
        pl.when(qi == n)(functools.partial(attend, (n + 1) * bq))


def _moba_prompt(qkv, batch, t):
    bq = MOBA_BLOCK
    nq = t // bq
    nb = t // MOBA_BLOCK
    kern = functools.partial(_moba_kernel, nb=nb, n_sel=min(MOBA_TOPK, nb))
    return pl.pallas_call(
        kern,
        grid=(batch, MOBA_HEADS, nq),
        in_specs=[
            pl.BlockSpec((bq, MOBA_HD), lambda b, h, i: (b * nq + i, h)),
            pl.BlockSpec((t, MOBA_HD), lambda b, h, i: (b, MOBA_HEADS + h)),
            pl.BlockSpec((t, MOBA_HD), lambda b, h, i: (b, 2 * MOBA_HEADS + h)),
        ],
        out_specs=pl.BlockSpec((bq, MOBA_HD), lambda b, h, i: (b * nq + i, h)),
        out_shape=jax.ShapeDtypeStruct((batch * t, HEAD_W), BF16),
        scratch_shapes=[pltpu.VMEM((t, MOBA_HD), BF16), pltpu.VMEM((t, MOBA_HD), BF16),
                        pltpu.VMEM((LANES, MOBA_HD), F32)],
        compiler_params=_cparams("parallel", "parallel", "arbitrary"),
        name="moba_prompt",
    )(qkv, qkv, qkv)


def _diff_lambda(lp_ref, lam_init):
    lp = lp_ref[...]
    a = jnp.sum(lp[0:1] * lp[1:2], axis=-1, keepdims=True)
    b = jnp.sum(lp[2:3] * lp[3:4], axis=-1, keepdims=True)
    return jnp.exp(a) - jnp.exp(b) + lam_init


def _diff_kernel(q_ref, k_ref, v_ref, lp_ref, g_ref, o_ref, kb_ref, vb_ref, *, lam_init):
    qi = pl.program_id(2)
    t = k_ref.shape[0]
    bq = q_ref.shape[0]

    @pl.when(qi == 0)
    def _():
        kb_ref[...] = k_ref[...].astype(BF16)
        vb_ref[...] = v_ref[...].astype(BF16)

    lam = _diff_lambda(lp_ref, lam_init)
    q = q_ref[...]
    lane = lax.broadcasted_iota(jnp.int32, q.shape, 1)
    q1 = jnp.where(lane < DIFF_HD, q, 0.0).astype(BF16)
    q2 = jnp.where(lane >= DIFF_HD, q, 0.0).astype(BF16)

    def attend(tk):
        row = lax.broadcasted_iota(jnp.int32, (bq, tk), 0) + (tk - bq)
        col = lax.broadcasted_iota(jnp.int32, (bq, tk), 1)
        causal = col <= row
        kb = kb_ref[0:tk, :]

        def softmax(qc):
            s = _dot_nt(qc, kb) * (DIFF_HD ** -0.5)
            s = jnp.where(causal, s, -jnp.inf)
            p = jnp.exp(s - jnp.max(s, axis=-1, keepdims=True))
            return p / jnp.sum(p, axis=-1, keepdims=True)

        a = softmax(q1) - lam * softmax(q2)
        o = _dot(a.astype(BF16), vb_ref[0:tk, :])
        o_ref[...] = (_rms(o, g_ref[...]) * (1.0 - lam_init)).astype(o_ref.dtype)

    for n in range(t // bq):
        pl.when(qi == n)(functools.partial(attend, (n + 1) * bq))


def _diff_prompt(qkv, lam_params, norm_gain, lam_init, batch, t):
    bq = 256
    nq = t // bq
    c0 = _C_DIFF // LANES
    kern = functools.partial(_diff_kernel, lam_init=lam_init)
    return pl.pallas_call(
        kern,
        grid=(batch, DIFF_HEADS, nq),
        in_specs=[
            pl.BlockSpec((bq, LANES), lambda b, h, i: (b * nq + i, c0 + h)),
            pl.BlockSpec((t, LANES), lambda b, h, i: (b, c0 + DIFF_HEADS + h)),
            pl.BlockSpec((t, LANES), lambda b, h, i: (b, c0 + 2 * DIFF_HEADS + h)),
            pl.BlockSpec((4, DIFF_HD), lambda b, h, i: (0, 0)),
            pl.BlockSpec((1, LANES), lambda b, h, i: (0, 0)),
        ],
        out_specs=pl.BlockSpec((bq, LANES), lambda b, h, i: (b * nq + i, h)),
        out_shape=jax.ShapeDtypeStruct((batch * t, HEAD_W), BF16),
        scratch_shapes=[pltpu.VMEM((t, LANES), BF16), pltpu.VMEM((t, LANES), BF16)],
        compiler_params=_cparams("parallel", "parallel", "arbitrary"),
        name="diff_prompt",
    )(qkv, qkv, qkv, lam_params, norm_gain.reshape(1, LANES))


def _l2norm_heads(x):
    parts = []
    for h in range(x.shape[1] // LANES):
        xh = x[:, h * LANES:(h + 1) * LANES]
        parts.append(xh * lax.rsqrt(jnp.sum(xh * xh, axis=-1, keepdims=True) + 1e-6))
    return jnp.concatenate(parts, axis=1)


def _conv_kernel(x_ref, w_ref, b_ref, o_ref, tail_ref):
    c = pl.program_id(1)
    x = x_ref[...]
    t = x.shape[0]
    row = lax.broadcasted_iota(jnp.int32, x.shape, 0)
    w = w_ref[...]
    acc = b_ref[...] + w[CONV_W - 1:CONV_W] * x
    for d in range(1, CONV_W):
        shifted = jnp.where(row >= d, pltpu.roll(x, d, 0), 0.0)
        acc = acc + w[CONV_W - 1 - d:CONV_W - d] * shifted
    y = _silu(acc)
    tail_ref[...] = x[t - (CONV_W - 1):, :]

    @pl.when(c < 2)
    def _():
        o_ref[...] = _l2norm_heads(y)

    @pl.when(c >= 2)
    def _():
        o_ref[...] = y


def _conv_prompt(x, w, b, batch, t):
    tc = HEAD_W
    return pl.pallas_call(
        _conv_kernel,
        grid=(batch, CONV_CH // tc),
        in_specs=[
            pl.BlockSpec((t, tc), lambda bi, c: (bi, c)),
            pl.BlockSpec((CONV_W, tc), lambda bi, c: (0, c)),
            pl.BlockSpec((1, tc), lambda bi, c: (0, c)),
        ],
        out_specs=[
            pl.BlockSpec((t, tc), lambda bi, c: (bi, c)),
            pl.BlockSpec((None, CONV_W - 1, tc), lambda bi, c: (bi, 0, c)),
        ],
        out_shape=[
            jax.ShapeDtypeStruct((batch * t, CONV_CH), F32),
            jax.ShapeDtypeStruct((batch, CONV_W - 1, CONV_CH), F32),
        ],
        compiler_params=_cparams("parallel", "parallel"),
        name="conv_prompt",
    )(x, w, b.reshape(1, CONV_CH))


def _conv_step_kernel(x_ref, buf_ref, w_ref, b_ref, o_ref, nb_ref):
    x = x_ref[...]
    w = w_ref[...]
    acc = b_ref[...] + w[CONV_W - 1:CONV_W] * x
    for j in range(CONV_W - 1):
        acc = acc + w[j:j + 1] * buf_ref[j]
    y = _silu(acc)
    qk = _l2norm_heads(y[:, :2 * HEAD_W])
    o_ref[...] = jnp.concatenate([qk, y[:, 2 * HEAD_W:]], axis=1)
    for j in range(CONV_W - 2):
        nb_ref[j] = buf_ref[j + 1]
    nb_ref[CONV_W - 2] = x


def _conv_step(x, buf, w, b):
    rows = x.shape[0]
    return pl.pallas_call(
        _conv_step_kernel,
        out_shape=[
            jax.ShapeDtypeStruct((rows, CONV_CH), F32),
            jax.ShapeDtypeStruct((CONV_W - 1, rows, CONV_CH), F32),
        ],
        compiler_params=pltpu.CompilerParams(vmem_limit_bytes=V7X_VMEM_LIMIT),
        name="conv_step",
    )(x, buf, w, b.reshape(1, CONV_CH))


def _chunk_iotas():
    r = lax.broadcasted_iota(jnp.int32, (CHUNK, CHUNK), 0)
    c = lax.broadcasted_iota(jnp.int32, (CHUNK, CHUNK), 1)
    return r, c


def _chunk_cumsum(col, r, c):
    row = jnp.sum(jnp.where(r == c, col, 0.0), axis=0, keepdims=True)
    cs_col = jnp.sum(jnp.where(c <= r, row, 0.0), axis=1, keepdims=True)
    cs_row = jnp.sum(jnp.where(r <= c, col, 0.0), axis=0, keepdims=True)
    return cs_col, cs_row


def _mm3(a, b):
    ah, al = a
    bh, bl = b
    return _dot(ah, bh) + (_dot(ah, bl) + _dot(al, bh))


_INV_BASE = 8


def _unit_lower_inverses(lows, r, c):
    def same_block(n):
        return (r // n) == (c // n)

    eye = (r == c).astype(F32)
    d1 = [jnp.where(same_block(_INV_BASE), low, 0.0) for low in lows]
    s1 = [_split_hi_lo(x) for x in d1]
    d2 = [_mm3(a, a) for a in s1]
    s2 = [_split_hi_lo(x) for x in d2]
    d3 = [_mm3(a, b) for a, b in zip(s1, s2)]
    d4 = [_mm3(b, b) for b in s2]
    inv = [eye - a + b - x for a, b, x in zip(d1, d2, d3)]
    inv = [x + _mm3(_split_hi_lo(x), _split_hi_lo(y)) for x, y in zip(inv, d4)]
    n = _INV_BASE
    while n < CHUNK:
        join = same_block(2 * n) & jnp.logical_not(same_block(n))
        si = [_split_hi_lo(x) for x in inv]
        mid = [_mm3(a, _split_hi_lo(jnp.where(join, low, 0.0))) for a, low in zip(si, lows)]
        inv = [x - _mm3(_split_hi_lo(y), a) for x, y, a in zip(inv, mid, si)]
        n *= 2
    return inv


def _bf(x):
    return x.astype(BF16)


def _gdn_kernel(qkv_ref, z_ref, sm_ref, alog_ref, bias_ref, gain_ref, s0_ref, o_ref, s_ref, *, n_valid):
    t = qkv_ref.shape[0]
    s_ref[...] = s0_ref[...]
    r, c = _chunk_iotas()
    incl = r >= c
    strict = r > c
    neg_a = -jnp.exp(alog_ref[...])

    def body(n, carry):
        r0 = pl.multiple_of(n * CHUNK, CHUNK)
        rows = pl.ds(r0, CHUNK)
        sm = sm_ref[rows, :]
        beta_all = _sigmoid(sm)
        g_all = neg_a * _softplus(sm + bias_ref[...])
        if n_valid < t:
            valid = (lax.broadcasted_iota(jnp.int32, (CHUNK, 1), 0) + r0) < n_valid
            beta_all = jnp.where(valid, beta_all, 0.0)
            g_all = jnp.where(valid, g_all, 0.0)
        heads = range(GDN_HEADS)
        hs = [slice(h * GDN_HD, (h + 1) * GDN_HD) for h in heads]
        q = [qkv_ref[rows, hs[h]] * (GDN_HD ** -0.5) for h in heads]
        k = [qkv_ref[rows, HEAD_W + h * GDN_HD:HEAD_W + (h + 1) * GDN_HD] for h in heads]
        v = [qkv_ref[rows, 2 * HEAD_W + h * GDN_HD:2 * HEAD_W + (h + 1) * GDN_HD] for h in heads]
        beta = [beta_all[:, h:h + 1] for h in heads]
        gc = [_chunk_cumsum(g_all[:, GDN_HEADS + h:GDN_HEADS + h + 1], r, c) for h in heads]
        gc_col = [x[0] for x in gc]
        gc_last = [x[CHUNK - 1:CHUNK, :] for x in gc_col]
        decay = [jnp.exp(jnp.where(incl, col - row, -jnp.inf)) for col, row in gc]
        kb = [k[h] * beta[h] for h in heads]
        k16 = [_bf(x) for x in k]
        kk = [_dot_nt(_bf(kb[h]), k16[h]) for h in heads]
        qk = [_dot_nt(_bf(q[h]), k16[h]) for h in heads]
        inv = _unit_lower_inverses([jnp.where(strict, kk[h] * decay[h], 0.0) for h in heads], r, c)
        egc = [jnp.exp(x) for x in gc_col]
        rhs = [jnp.concatenate([v[h] * beta[h], kb[h] * egc[h]], axis=1) for h in heads]
        sol = [_dot(_bf(inv[h]), _bf(rhs[h])) for h in heads]
        s = [s_ref[h] for h in heads]
        s16 = [_bf(x) for x in s]
        ws = [_dot(_bf(sol[h][:, GDN_HD:]), s16[h]) for h in heads]
        qs = [_dot(_bf(q[h] * egc[h]), s16[h]) for h in heads]
        v16 = [_bf(sol[h][:, :GDN_HD] - ws[h]) for h in heads]
        av = [_dot(_bf(qk[h] * decay[h]), v16[h]) for h in heads]
        kv = [_dot_tn(_bf(k[h] * jnp.exp(gc_last[h] - gc_col[h])), v16[h]) for h in heads]
        for h in heads:
            s_ref[h] = s[h] * jnp.exp(gc_last[h]) + kv[h]
            o = qs[h] + av[h]
            o_ref[rows, hs[h]] = (_rms(o, gain_ref[...]) * _silu(z_ref[rows, hs[h]])).astype(o_ref.dtype)
        return carry

    lax.fori_loop(0, t // CHUNK, body, 0)


def _gdn(conv_out, z, small, a_log, dt_bias, gain, s0, batch, t, n_valid):
    pad = jnp.zeros((LANES - 2 * GDN_HEADS,), F32)
    alog_vec = jnp.concatenate([jnp.zeros((GDN_HEADS,), F32), a_log, pad]).reshape(1, LANES)
    bias_vec = jnp.concatenate([jnp.zeros((GDN_HEADS,), F32), dt_bias, pad]).reshape(1, LANES)
    kern = functools.partial(_gdn_kernel, n_valid=n_valid)
    vec = pl.BlockSpec((1, LANES), lambda b: (0, 0))
    st = pl.BlockSpec((None, GDN_HEADS, GDN_HD, GDN_HD), lambda b: (b, 0, 0, 0))
    return pl.pallas_call(
        kern,
        grid=(batch,),
        in_specs=[
            pl.BlockSpec((t, 3 * HEAD_W), lambda b: (b, 0)),
            pl.BlockSpec((t, HEAD_W), lambda b: (b, 0)),
            pl.BlockSpec((t, LANES), lambda b: (b, 0)),
            vec, vec, vec, st,
        ],
        out_specs=[pl.BlockSpec((t, HEAD_W), lambda b: (b, 0)), st],
        out_shape=[
            jax.ShapeDtypeStruct((batch * t, HEAD_W), BF16),
            jax.ShapeDtypeStruct((batch, GDN_HEADS, GDN_HD, GDN_HD), F32),
        ],
        compiler_params=_cparams("parallel"),
        name="gdn",
    )(conv_out, z, small, alog_vec, bias_vec, gain.reshape(1, LANES), s0)


_DT_LANE = 2 * GDN_HEADS


def _ssd_kernel(x_ref, bc_ref, z_ref, sm_ref, alog_ref, bias_ref, d_ref, gain_ref, h0_ref,
                y_ref, h_ref, *, n_valid):
    t = x_ref.shape[0]
    h_ref[...] = h0_ref[...]
    r, c = _chunk_iotas()
    incl = r >= c
    neg_a = -jnp.exp(alog_ref[...])
    lane = lax.broadcasted_iota(jnp.int32, (CHUNK, LANES), 1)
    first = lane < SSM_HD
    srow = lax.broadcasted_iota(jnp.int32, (LANES, 1), 0) < SSM_HD
    pairs_per_group = SSM_HEADS // SSM_GROUPS // 2

    def body(n, carry):
        r0 = pl.multiple_of(n * CHUNK, CHUNK)
        rows = pl.ds(r0, CHUNK)
        dt_all = _softplus(sm_ref[rows, :] + bias_ref[...])
        if n_valid < t:
            valid = (lax.broadcasted_iota(jnp.int32, (CHUNK, 1), 0) + r0) < n_valid
            dt_all = jnp.where(valid, dt_all, 0.0)
        dta_all = dt_all * neg_a
        for g in range(SSM_GROUPS):
            bg = _bf(bc_ref[rows, g * SSM_STATE:(g + 1) * SSM_STATE])
            cg = _bf(bc_ref[rows, (SSM_GROUPS + g) * SSM_STATE:(SSM_GROUPS + g + 1) * SSM_STATE])
            cb = _dot_nt(cg, bg)
            ys = []
            for pr in range(pairs_per_group):
                p = g * pairs_per_group + pr
                ps = slice(p * LANES, (p + 1) * LANES)
                xp = x_ref[rows, ps]
                cols = []
                for hh in (2 * p, 2 * p + 1):
                    dt = dt_all[:, _DT_LANE + hh:_DT_LANE + hh + 1]
                    ac_col, ac_row = _chunk_cumsum(dta_all[:, _DT_LANE + hh:_DT_LANE + hh + 1], r, c)
                    lmask = jnp.exp(jnp.where(incl, ac_col - ac_row, -jnp.inf))
                    cols.append((dt, ac_col, lmask))
                (dt_a, ac_a, lm_a), (dt_b, ac_b, lm_b) = cols
                last_a = ac_a[CHUNK - 1:CHUNK, :]
                last_b = ac_b[CHUNK - 1:CHUNK, :]
                xdt = xp * jnp.where(first, dt_a, dt_b)
                y_intra = (_dot(_bf(cb * lm_a), _bf(jnp.where(first, xdt, 0.0)))
                           + _dot(_bf(cb * lm_b), _bf(jnp.where(first, 0.0, xdt))))
                hp = h_ref[p]
                y_inter = _dot_nt(cg, _bf(hp)) * jnp.where(first, jnp.exp(ac_a), jnp.exp(ac_b))
                x_dec = xdt * jnp.where(first, jnp.exp(last_a - ac_a), jnp.exp(last_b - ac_b))
                h_ref[p] = hp * jnp.where(srow, jnp.exp(last_a), jnp.exp(last_b)) + _dot_tn(_bf(x_dec), bg)
                y = y_intra + y_inter + d_ref[:, ps] * xp
                ys.append(y * _silu(z_ref[rows, ps]))
            ms = sum(jnp.sum(y * y, axis=-1, keepdims=True) for y in ys) / (len(ys) * LANES)
            scale = lax.rsqrt(ms + NORM_EPS)
            for pr, y in enumerate(ys):
                ps = slice((g * pairs_per_group + pr) * LANES, (g * pairs_per_group + pr + 1) * LANES)
                y_ref[rows, ps] = (y * scale * gain_ref[:, ps]).astype(y_ref.dtype)
        return carry

    lax.fori_loop(0, t // CHUNK, body, 0)


def _ssd(conv_out, z, small, a_log, dt_bias, d_skip, gain, h0, batch, t, n_valid):
    npair = SSM_HEADS // 2
    pad_l = jnp.zeros((_DT_LANE,), F32)
    pad_r = jnp.zeros((LANES - _DT_LANE - SSM_HEADS,), F32)
    alog_vec = jnp.concatenate([pad_l, a_log, pad_r]).reshape(1, LANES)
    bias_vec = jnp.concatenate([pad_l, dt_bias, pad_r]).reshape(1, LANES)
    d_vec = jnp.repeat(d_skip, SSM_HD).reshape(1, HEAD_W)
    kern = functools.partial(_ssd_kernel, n_valid=n_valid)
    vec = pl.BlockSpec((1, LANES), lambda b: (0, 0))
    wide = pl.BlockSpec((1, HEAD_W), lambda b: (0, 0))
    st = pl.BlockSpec((None, npair, LANES, SSM_STATE), lambda b: (b, 0, 0, 0))
    y, h = pl.pallas_call(
        kern,
        grid=(batch,),
        in_specs=[
            pl.BlockSpec((t, HEAD_W), lambda b: (b, 3)),
            pl.BlockSpec((t, HEAD_W), lambda b: (b, 4)),
            pl.BlockSpec((t, HEAD_W), lambda b: (b, 1)),
            pl.BlockSpec((t, LANES), lambda b: (b, 0)),
            vec, vec, wide, wide, st,
        ],
        out_specs=[pl.BlockSpec((t, HEAD_W), lambda b: (b, 0)), st],
        out_shape=[
            jax.ShapeDtypeStruct((batch * t, HEAD_W), BF16),
            jax.ShapeDtypeStruct((batch, npair, LANES, SSM_STATE), F32),
        ],
        compiler_params=_cparams("parallel"),
        name="ssd",
    )(conv_out, conv_out, z, small, alog_vec, bias_vec, d_vec, gain.reshape(1, HEAD_W),
      h0.reshape(batch, npair, LANES, SSM_STATE))
    return y, h.reshape(batch, SSM_HEADS, SSM_HD, SSM_STATE)


_ROWS = 8


_PAGE_ROWS = PAGE_SIZE * _ROWS
_PAGES_PER_STEP = 8


def _page_scores(q16, page16, valid, scale):
    return jnp.where(valid, _dot_nt(q16, page16) * scale, -jnp.inf)


def _page_specs(layer):
    def spec(k):
        return pl.BlockSpec((None, None, _PAGE_ROWS, LANES),
                            lambda b, j, pt: (pt[b, j * _PAGES_PER_STEP + k], layer, 0, 0))
    return [spec(k) for k in range(_PAGES_PER_STEP)]


def _moba_step_kernel(pt_ref, q_ref, kn_ref, vn_ref, *rest, n_sel, npg):
    pages = rest[:_PAGES_PER_STEP]
    o_ref, m_ref, l_ref, g_ref, acc_ref = rest[_PAGES_PER_STEP:]
    j = pl.program_id(1)
    scale = MOBA_HD ** -0.5
    lane = lax.broadcasted_iota(jnp.int32, (_ROWS, LANES), 1)

    @pl.when(j == 0)
    def _():
        m_ref[...] = jnp.full(m_ref.shape, -jnp.inf, F32)
        l_ref[...] = jnp.zeros_like(l_ref)
        g_ref[...] = jnp.zeros_like(g_ref)

    q = q_ref[...]
    q16 = _bf(q)
    col = lax.broadcasted_iota(jnp.int32, (_ROWS, _PAGE_ROWS), 1)
    sub = lax.broadcasted_iota(jnp.int32, (_ROWS, _PAGE_ROWS), 0)
    valid = (col % _ROWS) == (sub % MOBA_HEADS)
    m_all, l_all, g_all = m_ref[...], l_ref[...], g_ref[...]
    page16 = [_bf(p[...]) for p in pages]
    s = [_page_scores(q16, p16, valid, scale) for p16 in page16]
    m = [jnp.max(x, axis=-1, keepdims=True) for x in s]
    e = [jnp.exp(x - mx) for x, mx in zip(s, m)]
    acc = [_dot(_bf(pltpu.roll(x, MOBA_HEADS, 1)), p16) for x, p16 in zip(e, page16)]
    for k, page_ref in enumerate(pages):
        idx = j * _PAGES_PER_STEP + k
        acc_ref[idx] = acc[k]
        ksum = jnp.sum(page_ref[...].reshape(PAGE_SIZE, _ROWS, LANES), axis=0)
        here = lane == idx
        m_all = jnp.where(here, m[k], m_all)
        l_all = jnp.where(here, jnp.sum(e[k], axis=-1, keepdims=True), l_all)
        g_all = jnp.where(here, jnp.sum(ksum * q, axis=-1, keepdims=True), g_all)
    m_ref[...] = m_all
    l_ref[...] = l_all
    g_ref[...] = g_all

    @pl.when(j == pl.num_programs(1) - 1)
    def _():
        gb = g_all + jnp.where(lane % 2 == 0, pltpu.roll(g_all, LANES - 1, 1), pltpu.roll(g_all, 1, 1))
        gb = gb * (1.0 / MOBA_BLOCK)
        rank = jnp.zeros((_ROWS, LANES), jnp.int32)
        for jp in range(0, npg, 2):
            gcol = gb[:, jp:jp + 1]
            beats = (gcol > gb) | ((gcol == gb) & (jp // 2 < lane // 2))
            rank = rank + beats.astype(jnp.int32)
        sel = (rank < n_sel) & (lane < npg)
        s_own = jnp.sum(q16.astype(F32) * _bf(kn_ref[...]).astype(F32), axis=-1, keepdims=True) * scale
        m_tot = jnp.maximum(jnp.max(jnp.where(sel, m_all, -jnp.inf), axis=-1, keepdims=True), s_own)
        w = jnp.where(sel, jnp.exp(m_all - m_tot), 0.0)
        e_own = jnp.exp(s_own - m_tot)
        l_tot = jnp.sum(w * l_all, axis=-1, keepdims=True) + e_own
        acc = e_own * _bf(vn_ref[...]).astype(F32)
        for jj in range(npg):
            acc = acc + w[:, jj:jj + 1] * acc_ref[jj]
        o_ref[...] = acc / l_tot


def _moba_step(q8, k8, v8, cache, page_table, layer):
    rows, npg = page_table.shape
    assert npg % _PAGES_PER_STEP == 0 and npg <= LANES
    n_blocks = npg * PAGE_SIZE // MOBA_BLOCK + 1
    kern = functools.partial(_moba_step_kernel, n_sel=min(MOBA_TOPK, n_blocks), npg=npg)
    tile = pl.BlockSpec((None, _ROWS, LANES), lambda b, j, pt: (b, 0, 0))
    stat = pltpu.VMEM((_ROWS, LANES), F32)
    return pl.pallas_call(
        kern,
        grid_spec=pltpu.PrefetchScalarGridSpec(
            num_scalar_prefetch=1,
            grid=(rows, npg // _PAGES_PER_STEP),
            in_specs=[tile, tile, tile] + _page_specs(layer),
            out_specs=tile,
            scratch_shapes=[stat, stat, stat, pltpu.VMEM((npg, _ROWS, LANES), F32)],
        ),
        out_shape=jax.ShapeDtypeStruct((rows, _ROWS, LANES), F32),
        compiler_params=_cparams("parallel", "arbitrary"),
        name="moba_step",
    )(page_table, q8, k8, v8, *([cache] * _PAGES_PER_STEP))


def _diff_step_kernel(pt_ref, q_ref, kn_ref, vn_ref, lp_ref, gn_ref, *rest, lam_init):
    pages = rest[:_PAGES_PER_STEP]
    o_ref, m_ref, l_ref, acc_ref = rest[_PAGES_PER_STEP:]
    j = pl.program_id(1)
    scale = DIFF_HD ** -0.5

    @pl.when(j == 0)
    def _():
        m_ref[...] = jnp.full(m_ref.shape, -jnp.inf, F32)
        l_ref[...] = jnp.zeros_like(l_ref)
        acc_ref[...] = jnp.zeros_like(acc_ref)

    q16 = _bf(q_ref[...])
    col = lax.broadcasted_iota(jnp.int32, (_ROWS, _PAGE_ROWS), 1)
    sub = lax.broadcasted_iota(jnp.int32, (_ROWS, _PAGE_ROWS), 0)
    valid = (col % _ROWS) == (sub // 2)
    m_old, l_old = m_ref[...][:, 0:1], l_ref[...][:, 0:1]
    page16 = [_bf(p[...]) for p in pages]
    s = [_page_scores(q16, p16, valid, scale) for p16 in page16]
    m_run = m_old
    for x in s:
        m_run = jnp.maximum(m_run, jnp.max(x, axis=-1, keepdims=True))
    alpha = jnp.exp(m_old - m_run)
    e = [jnp.exp(x - m_run) for x in s]
    l_run = alpha * l_old + sum(jnp.sum(x, axis=-1, keepdims=True) for x in e)
    acc = alpha * acc_ref[...] + sum(_dot(_bf(pltpu.roll(x, DIFF_HEADS, 1)), p16) for x, p16 in zip(e, page16))
    m_ref[...] = jnp.broadcast_to(m_run, (_ROWS, LANES))
    l_ref[...] = jnp.broadcast_to(l_run, (_ROWS, LANES))
    acc_ref[...] = acc

    @pl.when(j == pl.num_programs(1) - 1)
    def _():
        lam = _diff_lambda(lp_ref, lam_init)
        s_new = jnp.sum(q16.astype(F32) * _bf(kn_ref[...]).astype(F32), axis=-1, keepdims=True) * scale
        m_new = jnp.maximum(m_run, s_new)
        alpha = jnp.exp(m_run - m_new)
        e = jnp.exp(s_new - m_new)
        o = (alpha * acc + e * _bf(vn_ref[...]).astype(F32)) / (alpha * l_run + e)
        d = o - lam * pltpu.roll(o, _ROWS - 1, 0)
        o_ref[...] = _rms(d, gn_ref[...]) * (1.0 - lam_init)


def _diff_step(q8, k8, v8, cache, page_table, layer, lam_params, norm_gain, lam_init):
    rows, npg = page_table.shape
    assert npg % _PAGES_PER_STEP == 0
    kern = functools.partial(_diff_step_kernel, lam_init=lam_init)
    tile = pl.BlockSpec((None, _ROWS, LANES), lambda b, j, pt: (b, 0, 0))
    stat = pltpu.VMEM((_ROWS, LANES), F32)
    return pl.pallas_call(
        kern,
        grid_spec=pltpu.PrefetchScalarGridSpec(
            num_scalar_prefetch=1,
            grid=(rows, npg // _PAGES_PER_STEP),
            in_specs=[tile, tile, tile,
                      pl.BlockSpec((4, DIFF_HD), lambda b, j, pt: (0, 0)),
                      pl.BlockSpec((1, LANES), lambda b, j, pt: (0, 0))] + _page_specs(layer),
            out_specs=tile,
            scratch_shapes=[stat, stat, stat],
        ),
        out_shape=jax.ShapeDtypeStruct((rows, _ROWS, LANES), F32),
        compiler_params=_cparams("parallel", "arbitrary"),
        name="diff_step",
    )(page_table, q8, k8, v8, lam_params, norm_gain.reshape(1, LANES), *([cache] * _PAGES_PER_STEP))


def _rope_tables(pos, hd, reps):
    rd = hd // 4
    half = rd // 2
    inv_freq = 1.0 / (ROPE_THETA ** (jnp.arange(half, dtype=F32) * (2.0 / rd)))
    ang = pos.astype(F32)[:, None] * inv_freq[None, :]
    cos = jnp.cos(ang)
    sin = jnp.sin(ang)
    n = pos.shape[0]
    zero = jnp.zeros((n, half), F32)
    c = jnp.concatenate([cos, cos, jnp.ones((n, hd - rd), F32)], axis=1)
    s1 = jnp.concatenate([zero, sin, jnp.zeros((n, hd - rd), F32)], axis=1)
    s2 = jnp.concatenate([-sin, zero, jnp.zeros((n, hd - rd), F32)], axis=1)
    return tuple(jnp.tile(a, (1, reps)) for a in (c, s1, s2))


def _pad_chunk(a):
    rows, n = a.shape
    return jnp.pad(a[:, None, :], ((0, 0), (0, CHUNK - 1), (0, 0))).reshape(rows * CHUNK, n)


def kernel(x_prompt, x_sample, cache_moba_kv, cache_diff_kv, state_gdn, state_ssm, state_conv, page_table, ffn1_norm, ffn1_w_gu, ffn1_w_down, mix_norm, w_in, conv_w, conv_b, diff_lambda, diff_norm, gdn_A_log, gdn_dt_bias, gdn_norm, ssm_A_log, ssm_dt_bias, ssm_D, ssm_norm, w_branch, w_out, ffn2_norm, ffn2_w_gu, ffn2_w_down, final_norm):
    bp, t, d = x_prompt.shape
    bs = x_sample.shape[0]
    depth = w_in.shape[0]
    assert d == D_MODEL and x_sample.shape[1] == 1 and t % MOBA_BLOCK == 0
    past_len = page_table.shape[1] * PAGE_SIZE
    assert past_len % MOBA_BLOCK == 0
    n_phys = cache_moba_kv.shape[0]
    cache_m = cache_moba_kv.reshape(n_phys, depth, _PAGE_ROWS, LANES)
    cache_d = cache_diff_kv.reshape(n_phys, depth, _PAGE_ROWS, LANES)
    comp_mask = (jnp.arange(LANES)[None, :] // DIFF_HD == jnp.arange(_ROWS)[:, None] % 2).astype(F32)

    tm_p = 512 if (bp * t) % 512 == 0 else MOBA_BLOCK
    pos_p = jnp.arange(t, dtype=jnp.int32)
    pos_s = jnp.full((bs,), past_len, jnp.int32)
    tables_p = _rope_tables(pos_p, MOBA_HD, 1) + _rope_tables(pos_p, DIFF_HD, 2)
    tables_s = _rope_tables(pos_s, MOBA_HD, 1) + _rope_tables(pos_s, DIFF_HD, 2)

    hp = x_prompt.reshape(bp * t, d)
    hs = x_sample.reshape(bs, d)
    outs = {k: [] for k in ("moba_p", "moba_s", "diff_p", "diff_s", "gdn_p", "gdn_s", "ssm_p", "ssm_s",
                            "conv_p", "conv_s")}
    y_p = y_s = None
    for l in range(depth):
        last = l == depth - 1
        lam_init = 0.8 - 0.6 * math.exp(-0.3 * l)
        wi = w_in[l]
        w_qkv = wi[:, _C_MOBA:_C_CONV].astype(BF16)
        w_conv = wi[:, _C_CONV:_C_GDNZ].astype(BF16)
        w_z = jnp.concatenate([wi[:, _C_GDNZ:_C_GDNB], wi[:, _C_SSMZ:_C_SSMDT]], axis=1).astype(BF16)
        w_small = jnp.concatenate([wi[:, _C_GDNB:_C_SSMZ], wi[:, _C_SSMDT:_C_GATE],
                                   jnp.zeros((d, LANES - 4 * GDN_HEADS), F32)], axis=1).astype(BF16)
        w_gate = jnp.transpose(wi[:, _C_GATE:].reshape(d, N_BRANCH, d), (1, 0, 2)).astype(BF16)
        w_br = w_branch[l].astype(BF16)
        w_o = w_out[l].astype(BF16)
        wgu1 = ffn1_w_gu[l].astype(BF16)
        wd1 = ffn1_w_down[l].astype(BF16)
        wgu2 = ffn2_w_gu[l].astype(BF16)
        wd2 = ffn2_w_down[l].astype(BF16)
        post_gain = final_norm if last else ffn1_norm[l + 1]
        post_dtype = F32 if last else BF16

        def dense_in(h, tm, tables, rows_per_seq):
            h1, u = _ffn(h, ffn1_norm[l], wgu1, wd1, mix_norm[l], BF16, tm)
            qkv = _qkv_proj(u, w_qkv, tables, tm, rows_per_seq)
            conv_in = _matmul(u, w_conv, tm, HEAD_W)
            z = _matmul(u, w_z, tm, HEAD_W)
            small = _matmul(u, w_small, tm, LANES)
            return h1, u, qkv, conv_in, z, small

        def dense_out(h1, u, branches, tm):
            merged = _merge(u, branches, w_gate, w_br, tm)
            h2 = _outproj(merged, w_o, h1, tm)
            return _ffn(h2, ffn2_norm[l], wgu2, wd2, post_gain, post_dtype, tm)

        h1, u, qkv, conv_in, z, small = dense_in(hp, tm_p, tables_p, t)
        o_moba = _moba_prompt(qkv, bp, t)
        o_diff = _diff_prompt(qkv, diff_lambda[l], diff_norm[l], lam_init, bp, t)
        conv_out, conv_new = _conv_prompt(conv_in, conv_w[l], conv_b[l], bp, t)
        o_gdn, s_new = _gdn(conv_out, z, small, gdn_A_log[l], gdn_dt_bias[l], gdn_norm[l],
                            jnp.zeros((bp, GDN_HEADS, GDN_HD, GDN_HD), F32), bp, t, t)
        y_ssm, h_new = _ssd(conv_out, z, small, ssm_A_log[l], ssm_dt_bias[l], ssm_D[l], ssm_norm[l],
                            jnp.zeros((bp, SSM_HEADS, SSM_HD, SSM_STATE), F32), bp, t, t)
        hp, y_p = dense_out(h1, u, (o_moba, o_diff, o_gdn, y_ssm), tm_p)
        outs["moba_p"].append(qkv[:, HEAD_W:3 * HEAD_W].reshape(bp, t, 2, MOBA_HEADS, MOBA_HD))
        outs["diff_p"].append(qkv[:, _C_DIFF + HEAD_W:].reshape(bp, t, 2, DIFF_HEADS, 2 * DIFF_HD))
        outs["gdn_p"].append(s_new)
        outs["ssm_p"].append(h_new)
        outs["conv_p"].append(conv_new)

        h1, u, qkv, conv_in, z, small = dense_in(hs, bs, tables_s, 1)
        heads = qkv.reshape(bs, 6, 4, LANES)
        mq, mk, mv = (jnp.tile(heads[:, i], (1, 2, 1)) for i in range(3))
        dq, dk, dv = (jnp.repeat(heads[:, i], 2, axis=1) for i in range(3, 6))
        o_moba = _moba_step(mq, mk, mv, cache_m, page_table, l)[:, :MOBA_HEADS].reshape(bs, HEAD_W).astype(BF16)
        o_diff = _diff_step(dq * comp_mask, dk, dv, cache_d, page_table, l, diff_lambda[l], diff_norm[l], lam_init)
        o_diff = o_diff[:, ::2].reshape(bs, HEAD_W).astype(BF16)
        conv_out, buf_new = _conv_step(conv_in, jnp.moveaxis(state_conv[l], 1, 0), conv_w[l], conv_b[l])
        conv_pad, z_pad, small_pad = _pad_chunk(conv_out), _pad_chunk(z), _pad_chunk(small)
        o_gdn, s_new = _gdn(conv_pad, z_pad, small_pad, gdn_A_log[l], gdn_dt_bias[l], gdn_norm[l],
                            state_gdn[l], bs, CHUNK, 1)
        y_ssm, h_new = _ssd(conv_pad, z_pad, small_pad, ssm_A_log[l], ssm_dt_bias[l], ssm_D[l], ssm_norm[l],
                            state_ssm[l], bs, CHUNK, 1)
        o_gdn = o_gdn.reshape(bs, CHUNK, HEAD_W)[:, 0]
        y_ssm = y_ssm.reshape(bs, CHUNK, HEAD_W)[:, 0]
        hs, y_s = dense_out(h1, u, (o_moba, o_diff, o_gdn, y_ssm), bs)
        outs["moba_s"].append(qkv[:, HEAD_W:3 * HEAD_W].reshape(bs, 1, 2, MOBA_HEADS, MOBA_HD))
        outs["diff_s"].append(qkv[:, _C_DIFF + HEAD_W:].reshape(bs, 1, 2, DIFF_HEADS, 2 * DIFF_HD))
        outs["gdn_s"].append(s_new)
        outs["ssm_s"].append(h_new)
        outs["conv_s"].append(jnp.moveaxis(buf_new, 0, 1))

    return (y_p.reshape(bp, t, d), y_s.reshape(bs, 1, d),
            jnp.stack(outs["moba_p"], axis=1), jnp.stack(outs["moba_s"], axis=1),
            jnp.stack(outs["diff_p"], axis=1), jnp.stack(outs["diff_s"], axis=1),
            jnp.stack(outs["gdn_p"]), jnp.stack(outs["gdn_s"]),
            jnp.stack(outs["ssm_p"]), jnp.stack(outs["ssm_s"]),
            jnp.stack(outs["conv_p"]), jnp.stack(outs["conv_s"]))
```

```python
import functools
import math

import jax
import jax.numpy as jnp
from jax import lax
from jax.experimental import pallas as pl
from jax.experimental.pallas import tpu as pltpu

F32 = jnp.float32
BF16 = jnp.bfloat16

D_MODEL = 2048
D_FF = 5632
NORM_EPS = 1e-6
ROPE_THETA = 500000.0
PAGE_SIZE = 128

MOBA_HEADS = 4
MOBA_HD = 128
MOBA_BLOCK = 256
MOBA_TOPK = 3
DIFF_HEADS = 4
DIFF_HD = 64
GDN_HEADS = 4
GDN_HD = 128
SSM_HEADS = 8
SSM_HD = 64
SSM_GROUPS = 2
SSM_STATE = 128
CHUNK = 64
CONV_W = 4
HEAD_W = 512
CONV_CH = 2560
N_BRANCH = 4

LANES = 128
V7X_VMEM_LIMIT = 56 * 1024 * 1024

_C_MOBA = 0
_C_DIFF = 1536
_C_CONV = 3072
_C_GDNZ = 5632
_C_GDNB = 6144
_C_GDNA = 6148
_C_SSMZ = 6152
_C_SSMDT = 6664
_C_GATE = 6672


def _cparams(*sem):
    return pltpu.CompilerParams(dimension_semantics=sem, vmem_limit_bytes=V7X_VMEM_LIMIT)


def _rms(x, gain):
    y = x * lax.rsqrt(jnp.mean(x * x, axis=-1, keepdims=True) + NORM_EPS)
    return y * gain


def _dot(a, b):
    return jnp.dot(a, b, preferred_element_type=F32)


def _dot_nt(a, b):
    return lax.dot_general(a, b, (((1,), (1,)), ((), ())), preferred_element_type=F32)


def _dot_tn(a, b):
    return lax.dot_general(a, b, (((0,), (0,)), ((), ())), preferred_element_type=F32)


def _softplus(x):
    return jnp.maximum(x, 0.0) + jnp.log(1.0 + jnp.exp(-jnp.abs(x)))


def _sigmoid(x):
    return 1.0 / (1.0 + jnp.exp(-x))


def _silu(x):
    return x * _sigmoid(x)


_FFN_SLICES = 2


def _ffn_kernel(x_ref, g_ref, wg_ref, wu_ref, wd_ref, pg_ref, o_ref, p_ref, xn_ref):
    f = pl.program_id(1)

    @pl.when(f == 0)
    def _():
        x = x_ref[...]
        xn_ref[...] = _rms(x, g_ref[...]).astype(BF16)
        o_ref[...] = x

    xn = xn_ref[...]
    tf = wg_ref.shape[1]
    w = tf // _FFN_SLICES
    gu = [(_dot(xn, wg_ref[:, c * w:(c + 1) * w]), _dot(xn, wu_ref[:, c * w:(c + 1) * w]))
          for c in range(_FFN_SLICES)]
    acc = None
    for c, (gate, up) in enumerate(gu):
        act = (0.5 * _silu(gate) * up).astype(BF16)
        part = _dot(act, wd_ref[c * w:(c + 1) * w, :])
        acc = part if acc is None else acc + part
    o_ref[...] += acc

    @pl.when(f == pl.num_programs(1) - 1)
    def _():
        p_ref[...] = _rms(o_ref[...], pg_ref[...]).astype(p_ref.dtype)


def _ffn(x, gain, w_gu, w_down, post_gain, post_dtype, tm, tf=512):
    m = x.shape[0]
    nf = D_FF // tf
    return pl.pallas_call(
        _ffn_kernel,
        grid=(m // tm, nf),
        in_specs=[
            pl.BlockSpec((tm, D_MODEL), lambda i, f: (i, 0)),
            pl.BlockSpec((1, D_MODEL), lambda i, f: (0, 0)),
            pl.BlockSpec((D_MODEL, tf), lambda i, f: (0, f)),
            pl.BlockSpec((D_MODEL, tf), lambda i, f: (0, f + nf)),
            pl.BlockSpec((tf, D_MODEL), lambda i, f: (f, 0)),
            pl.BlockSpec((1, D_MODEL), lambda i, f: (0, 0)),
        ],
        out_specs=[
            pl.BlockSpec((tm, D_MODEL), lambda i, f: (i, 0)),
            pl.BlockSpec((tm, D_MODEL), lambda i, f: (i, 0)),
        ],
        out_shape=[
            jax.ShapeDtypeStruct((m, D_MODEL), F32),
            jax.ShapeDtypeStruct((m, D_MODEL), post_dtype),
        ],
        scratch_shapes=[pltpu.VMEM((tm, D_MODEL), BF16)],
        compiler_params=_cparams("parallel", "arbitrary"),
        name="ffn",
    )(x, gain.reshape(1, D_MODEL), w_gu, w_gu, w_down, post_gain.reshape(1, D_MODEL))


def _rope(x, c_ref, s1_ref, s2_ref, half):
    n = x.shape[1]
    reps = n // LANES
    c = jnp.concatenate([c_ref[...]] * reps, axis=1)
    s1 = jnp.concatenate([s1_ref[...]] * reps, axis=1)
    s2 = jnp.concatenate([s2_ref[...]] * reps, axis=1)
    return x * c + pltpu.roll(x, half, 1) * s1 + pltpu.roll(x, n - half, 1) * s2


_P_CONV = 6
_P_GDNZ = 11
_P_SSMZ = 12
_P_SMALL = 13
_P_TILES = 14


def _in_proj_kernel(u_ref, w_ref, mc_ref, ms1_ref, ms2_ref, dc_ref, ds1_ref, ds2_ref, o_ref):
    j = pl.program_id(1)
    w = w_ref.shape[1] // 2

    def tile(epilogue):
        u = u_ref[...]
        halves = [_dot(u, w_ref[:, c * w:(c + 1) * w]) for c in range(2)]
        for c, acc in enumerate(halves):
            o_ref[:, c * w:(c + 1) * w] = epilogue(acc)

    @pl.when(j < 2)
    def _():
        tile(lambda acc: _rope(acc, mc_ref, ms1_ref, ms2_ref, MOBA_HD // 8))

    @pl.when((j == 3) | (j == 4))
    def _():
        tile(lambda acc: _rope(acc, dc_ref, ds1_ref, ds2_ref, DIFF_HD // 8))

    @pl.when((j == 2) | (j >= 5))
    def _():
        tile(lambda acc: acc)


def _in_proj(u, w, tables, tm, rows_per_seq):
    m = u.shape[0]
    tn = HEAD_W
    nt = rows_per_seq // tm if rows_per_seq >= tm else 1
    tspec = pl.BlockSpec((tm, LANES), lambda i, j: (i % nt, 0))
    return pl.pallas_call(
        _in_proj_kernel,
        grid=(m // tm, _P_TILES),
        in_specs=[
            pl.BlockSpec((tm, D_MODEL), lambda i, j: (i, 0)),
            pl.BlockSpec((D_MODEL, tn), lambda i, j: (0, j)),
        ] + [tspec] * 6,
        out_specs=pl.BlockSpec((tm, tn), lambda i, j: (i, j)),
        out_shape=jax.ShapeDtypeStruct((m, _P_TILES * tn), F32),
        compiler_params=_cparams("parallel", "arbitrary"),
        name="in_proj",
    )(u, w, *tables)


def _merge_kernel(u_ref, b0_ref, b1_ref, b2_ref, b3_ref, wg_ref, wb_ref, o_ref):
    u = u_ref[...]
    acc = None
    for i, b_ref in enumerate((b0_ref, b1_ref, b2_ref, b3_ref)):
        gate = _sigmoid(_dot(u, wg_ref[i]))
        term = gate * _dot(b_ref[...], wb_ref[i])
        acc = term if acc is None else acc + term
    o_ref[...] = acc.astype(o_ref.dtype)


def _merge(u, branches, w_gate, w_branch, tm, tn=256):
    m = u.shape[0]
    bspec = pl.BlockSpec((tm, HEAD_W), lambda i, j: (i, 0))
    return pl.pallas_call(
        _merge_kernel,
        grid=(m // tm, D_MODEL // tn),
        in_specs=[pl.BlockSpec((tm, D_MODEL), lambda i, j: (i, 0)), bspec, bspec, bspec, bspec,
                  pl.BlockSpec((N_BRANCH, D_MODEL, tn), lambda i, j: (0, 0, j)),
                  pl.BlockSpec((N_BRANCH, HEAD_W, tn), lambda i, j: (0, 0, j))],
        out_specs=pl.BlockSpec((tm, tn), lambda i, j: (i, j)),
        out_shape=jax.ShapeDtypeStruct((m, D_MODEL), BF16),
        compiler_params=_cparams("parallel", "arbitrary"),
        name="merge",
    )(u, *branches, w_gate, w_branch)


def _outproj_kernel(a_ref, w_ref, h_ref, o_ref):
    o_ref[...] = h_ref[...] + _dot(a_ref[...], w_ref[...])


def _outproj(a, w, h, tm, tn=512):
    m = a.shape[0]
    return pl.pallas_call(
        _outproj_kernel,
        grid=(m // tm, D_MODEL // tn),
        in_specs=[pl.BlockSpec((tm, D_MODEL), lambda i, j: (i, 0)),
                  pl.BlockSpec((D_MODEL, tn), lambda i, j: (0, j)),
                  pl.BlockSpec((tm, tn), lambda i, j: (i, j))],
        out_specs=pl.BlockSpec((tm, tn), lambda i, j: (i, j)),
        out_shape=jax.ShapeDtypeStruct((m, D_MODEL), F32),
        compiler_params=_cparams("parallel", "arbitrary"),
        name="outproj",
    )(a, w, h)


def _split_hi_lo(x):
    hi = x.astype(BF16)
    lo = (x - hi.astype(F32)).astype(BF16)
    return hi, lo


def _moba_kernel(q_ref, k_ref, v_ref, o_ref, kb_ref, vb_ref, km_ref, *, nb, n_sel):
    qi = pl.program_id(2)
    t = k_ref.shape[0]
    bq = q_ref.shape[0]

    @pl.when(qi == 0)
    def _():
        k = k_ref[...]
        kb_ref[...] = k.astype(BF16)
        vb_ref[...] = v_ref[...].astype(BF16)
        km_ref[...] = jnp.zeros_like(km_ref)
        for n in range(nb):
            km_ref[n:n + 1, :] = jnp.mean(k[n * MOBA_BLOCK:(n + 1) * MOBA_BLOCK, :], axis=0, keepdims=True)

    q = q_ref[...]
    qh, ql = _split_hi_lo(q)
    kh, kl = _split_hi_lo(km_ref[...])
    gate = _dot_nt(qh, kh) + (_dot_nt(qh, kl) + _dot_nt(ql, kh))
    lane = lax.broadcasted_iota(jnp.int32, gate.shape, 1)
    past = lane < qi
    gate = jnp.where(past, gate, -jnp.inf)
    rank = jnp.zeros(gate.shape, jnp.int32)
    for m in range(nb):
        gm = gate[:, m:m + 1]
        beats = (gm > gate) | ((gm == gate) & (m < lane))
        rank = rank + beats.astype(jnp.int32)
    sel = (past & (rank < n_sel)).astype(F32).astype(BF16)
    q16 = q.astype(BF16)

    scale = MOBA_HD ** -0.5
    lrow = lax.broadcasted_iota(jnp.int32, (bq, bq), 0)
    lcol = lax.broadcasted_iota(jnp.int32, (bq, bq), 1)

    def attend(tk):
        tp = tk - bq
        s_own = jnp.where(lcol <= lrow, _dot_nt(q16, kb_ref[tp:tk, :]) * scale, -jnp.inf)
        m = jnp.max(s_own, axis=-1, keepdims=True)
        if tp:
            erow = lax.broadcasted_iota(jnp.int32, (LANES, tp), 0)
            ecol = lax.broadcasted_iota(jnp.int32, (LANES, tp), 1)
            expand = (ecol // MOBA_BLOCK == erow).astype(F32).astype(BF16)
            s_past = jnp.where(_dot(sel, expand) > 0.5, _dot_nt(q16, kb_ref[0:tp, :]) * scale, -jnp.inf)
            m = jnp.maximum(m, jnp.max(s_past, axis=-1, keepdims=True))
        e_own = jnp.exp(s_own - m)
        l = jnp.sum(e_own, axis=-1, keepdims=True)
        acc = _dot(e_own.astype(BF16), vb_ref[tp:tk, :])
        if tp:
            e_past = jnp.exp(s_past - m)
            l = l + jnp.sum(e_past, axis=-1, keepdims=True)
            acc = acc + _dot(e_past.astype(BF16), vb_ref[0:tp, :])
        o_ref[...] = (acc / l).astype(o_ref.dtype)

    for n in range(t // bq):
        pl.when(qi == n)(functools.partial(attend, (n + 1) * bq))


def _moba_prompt(qkv, batch, t):
    bq = MOBA_BLOCK
    nq = t // bq
    nb = t // MOBA_BLOCK
    kern = functools.partial(_moba_kernel, nb=nb, n_sel=min(MOBA_TOPK, nb))
    return pl.pallas_call(
        kern,
        grid=(batch, MOBA_HEADS, nq),
        in_specs=[
            pl.BlockSpec((bq, MOBA_HD), lambda b, h, i: (b * nq + i, h)),
            pl.BlockSpec((t, MOBA_HD), lambda b, h, i: (b, MOBA_HEADS + h)),
            pl.BlockSpec((t, MOBA_HD), lambda b, h, i: (b, 2 * MOBA_HEADS + h)),
        ],
        out_specs=pl.BlockSpec((bq, MOBA_HD), lambda b, h, i: (b * nq + i, h)),
        out_shape=jax.ShapeDtypeStruct((batch * t, HEAD_W), BF16),
        scratch_shapes=[pltpu.VMEM((t, MOBA_HD), BF16), pltpu.VMEM((t, MOBA_HD), BF16),
                        pltpu.VMEM((LANES, MOBA_HD), F32)],
        compiler_params=_cparams("parallel", "parallel", "arbitrary"),
        name="moba_prompt",
    )(qkv, qkv, qkv)


def _diff_lambda(lp_ref, lam_init):
    lp = lp_ref[...]
    a = jnp.sum(lp[0:1] * lp[1:2], axis=-1, keepdims=True)
    b = jnp.sum(lp[2:3] * lp[3:4], axis=-1, keepdims=True)
    return jnp.exp(a) - jnp.exp(b) + lam_init


def _diff_kernel(q_ref, k_ref, v_ref, lp_ref, g_ref, o_ref, kb_ref, vb_ref, *, lam_init):
    qi = pl.program_id(2)
    t = k_ref.shape[0]
    bq = q_ref.shape[0]

    @pl.when(qi == 0)
    def _():
        kb_ref[...] = k_ref[...].astype(BF16)
        vb_ref[...] = v_ref[...].astype(BF16)

    lam = _diff_lambda(lp_ref, lam_init)
    q = q_ref[...]
    lane = lax.broadcasted_iota(jnp.int32, q.shape, 1)
    q = q * (DIFF_HD ** -0.5)
    q1 = jnp.where(lane < DIFF_HD, q, 0.0).astype(BF16)
    q2 = jnp.where(lane >= DIFF_HD, q, 0.0).astype(BF16)
    lrow = lax.broadcasted_iota(jnp.int32, (bq, bq), 0)
    lcol = lax.broadcasted_iota(jnp.int32, (bq, bq), 1)

    def attend(tk):
        tp = tk - bq
        qs = (q1, q2)
        s_own = [jnp.where(lcol <= lrow, _dot_nt(qc, kb_ref[tp:tk, :]), -jnp.inf) for qc in qs]
        m = [jnp.max(s, axis=-1, keepdims=True) for s in s_own]
        if tp:
            s_past = [_dot_nt(qc, kb_ref[0:tp, :]) for qc in qs]
            m = [jnp.maximum(mx, jnp.max(s, axis=-1, keepdims=True)) for mx, s in zip(m, s_past)]
        e_own = [jnp.exp(s - mx) for s, mx in zip(s_own, m)]
        l = [jnp.sum(e, axis=-1, keepdims=True) for e in e_own]
        acc = [_dot(e.astype(BF16), vb_ref[tp:tk, :]) for e in e_own]
        if tp:
            e_past = [jnp.exp(s - mx) for s, mx in zip(s_past, m)]
            l = [x + jnp.sum(e, axis=-1, keepdims=True) for x, e in zip(l, e_past)]
            acc = [a + _dot(e.astype(BF16), vb_ref[0:tp, :]) for a, e in zip(acc, e_past)]
        o = acc[0] / l[0] - lam * (acc[1] / l[1])
        o_ref[...] = (_rms(o, g_ref[...]) * (1.0 - lam_init)).astype(o_ref.dtype)

    for n in range(t // bq):
        pl.when(qi == n)(functools.partial(attend, (n + 1) * bq))


def _diff_prompt(qkv, lam_params, norm_gain, lam_init, batch, t):
    bq = 512 if t % 512 == 0 else 256
    nq = t // bq
    c0 = _C_DIFF // LANES
    kern = functools.partial(_diff_kernel, lam_init=lam_init)
    return pl.pallas_call(
        kern,
        grid=(batch, DIFF_HEADS, nq),
        in_specs=[
            pl.BlockSpec((bq, LANES), lambda b, h, i: (b * nq + i, c0 + h)),
            pl.BlockSpec((t, LANES), lambda b, h, i: (b, c0 + DIFF_HEADS + h)),
            pl.BlockSpec((t, LANES), lambda b, h, i: (b, c0 + 2 * DIFF_HEADS + h)),
            pl.BlockSpec((4, DIFF_HD), lambda b, h, i: (0, 0)),
            pl.BlockSpec((1, LANES), lambda b, h, i: (0, 0)),
        ],
        out_specs=pl.BlockSpec((bq, LANES), lambda b, h, i: (b * nq + i, h)),
        out_shape=jax.ShapeDtypeStruct((batch * t, HEAD_W), BF16),
        scratch_shapes=[pltpu.VMEM((t, LANES), BF16), pltpu.VMEM((t, LANES), BF16)],
        compiler_params=_cparams("parallel", "parallel", "arbitrary"),
        name="diff_prompt",
    )(qkv, qkv, qkv, lam_params, norm_gain.reshape(1, LANES))


def _l2norm_heads(x):
    parts = []
    for h in range(x.shape[1] // LANES):
        xh = x[:, h * LANES:(h + 1) * LANES]
        parts.append(xh * lax.rsqrt(jnp.sum(xh * xh, axis=-1, keepdims=True) + 1e-6))
    return jnp.concatenate(parts, axis=1)


def _conv_kernel(x_ref, w_ref, b_ref, o_ref, tail_ref):
    c = pl.program_id(1)
    x = x_ref[...]
    t = x.shape[0]
    row = lax.broadcasted_iota(jnp.int32, x.shape, 0)
    w = w_ref[...]
    acc = b_ref[...] + w[CONV_W - 1:CONV_W] * x
    for d in range(1, CONV_W):
        shifted = jnp.where(row >= d, pltpu.roll(x, d, 0), 0.0)
        acc = acc + w[CONV_W - 1 - d:CONV_W - d] * shifted
    y = _silu(acc)
    tail_ref[...] = x[t - (CONV_W - 1):, :]

    @pl.when(c < 2)
    def _():
        o_ref[...] = _l2norm_heads(y)

    @pl.when(c >= 2)
    def _():
        o_ref[...] = y


def _conv_prompt(x, w, b, batch, t, tile0):
    tc = HEAD_W
    return pl.pallas_call(
        _conv_kernel,
        grid=(batch, CONV_CH // tc),
        in_specs=[
            pl.BlockSpec((t, tc), lambda bi, c: (bi, tile0 + c)),
            pl.BlockSpec((CONV_W, tc), lambda bi, c: (0, c)),
            pl.BlockSpec((1, tc), lambda bi, c: (0, c)),
        ],
        out_specs=[
            pl.BlockSpec((t, tc), lambda bi, c: (bi, c)),
            pl.BlockSpec((None, CONV_W - 1, tc), lambda bi, c: (bi, 0, c)),
        ],
        out_shape=[
            jax.ShapeDtypeStruct((batch * t, CONV_CH), F32),
            jax.ShapeDtypeStruct((batch, CONV_W - 1, CONV_CH), F32),
        ],
        compiler_params=_cparams("parallel", "parallel"),
        name="conv_prompt",
    )(x, w, b.reshape(1, CONV_CH))


def _conv_step_kernel(x_ref, buf_ref, w_ref, b_ref, o_ref, nb_ref):
    x = x_ref[...]
    w = w_ref[...]
    acc = b_ref[...] + w[CONV_W - 1:CONV_W] * x
    for j in range(CONV_W - 1):
        acc = acc + w[j:j + 1] * buf_ref[j]
    y = _silu(acc)
    qk = _l2norm_heads(y[:, :2 * HEAD_W])
    o_ref[...] = jnp.concatenate([qk, y[:, 2 * HEAD_W:]], axis=1)
    for j in range(CONV_W - 2):
        nb_ref[j] = buf_ref[j + 1]
    nb_ref[CONV_W - 2] = x


def _conv_step(x, buf, w, b):
    rows = x.shape[0]
    return pl.pallas_call(
        _conv_step_kernel,
        out_shape=[
            jax.ShapeDtypeStruct((rows, CONV_CH), F32),
            jax.ShapeDtypeStruct((CONV_W - 1, rows, CONV_CH), F32),
        ],
        compiler_params=pltpu.CompilerParams(vmem_limit_bytes=V7X_VMEM_LIMIT),
        name="conv_step",
    )(x, buf, w, b.reshape(1, CONV_CH))


def _chunk_iotas():
    r = lax.broadcasted_iota(jnp.int32, (CHUNK, CHUNK), 0)
    c = lax.broadcasted_iota(jnp.int32, (CHUNK, CHUNK), 1)
    return r, c


def _chunk_cumsum(col, r, c):
    row = jnp.sum(jnp.where(r == c, col, 0.0), axis=0, keepdims=True)
    cs_col = jnp.sum(jnp.where(c <= r, row, 0.0), axis=1, keepdims=True)
    cs_row = jnp.sum(jnp.where(r <= c, col, 0.0), axis=0, keepdims=True)
    return cs_col, cs_row


def _mm3(a, b):
    ah, al = a
    bh, bl = b
    return _dot(ah, bh) + (_dot(ah, bl) + _dot(al, bh))


_INV_BASE = 8


def _unit_lower_inverses(lows, r, c):
    def same_block(n):
        return (r // n) == (c // n)

    eye = (r == c).astype(F32)
    d1 = [jnp.where(same_block(_INV_BASE), low, 0.0) for low in lows]
    s1 = [_split_hi_lo(x) for x in d1]
    d2 = [_mm3(a, a) for a in s1]
    s2 = [_split_hi_lo(x) for x in d2]
    d3 = [_mm3(a, b) for a, b in zip(s1, s2)]
    d4 = [_mm3(b, b) for b in s2]
    inv = [eye - a + b - x for a, b, x in zip(d1, d2, d3)]
    inv = [x + _mm3(_split_hi_lo(x), _split_hi_lo(y)) for x, y in zip(inv, d4)]
    n = _INV_BASE
    while n < CHUNK:
        join = same_block(2 * n) & jnp.logical_not(same_block(n))
        si = [_split_hi_lo(x) for x in inv]
        mid = [_mm3(a, _split_hi_lo(jnp.where(join, low, 0.0))) for a, low in zip(si, lows)]
        inv = [x - _mm3(_split_hi_lo(y), a) for x, y, a in zip(inv, mid, si)]
        n *= 2
    return inv


def _bf(x):
    return x.astype(BF16)


def _gdn_kernel(qkv_ref, z_ref, sm_ref, alog_ref, bias_ref, gain_ref, s0_ref, o_ref, s_ref, *, n_valid):
    t = qkv_ref.shape[0]
    s_ref[...] = s0_ref[...]
    r, c = _chunk_iotas()
    incl = r >= c
    strict = r > c
    neg_a = -jnp.exp(alog_ref[...])

    n_chunks = t // CHUNK
    per_iter = 2 if n_chunks % 2 == 0 else 1
    heads = range(GDN_HEADS)
    hs = [slice(h * GDN_HD, (h + 1) * GDN_HD) for h in heads]

    def body(n, carry):
        rows, q, k, v, beta, gc_col, gc_last, decay, kb, k16 = ([] for _ in range(10))
        for ci in range(per_iter):
            r0 = pl.multiple_of((n * per_iter + ci) * CHUNK, CHUNK)
            rw = pl.ds(r0, CHUNK)
            sm = sm_ref[rw, :]
            beta_all = _sigmoid(sm)
            g_all = neg_a * _softplus(sm + bias_ref[...])
            if n_valid < t:
                valid = (lax.broadcasted_iota(jnp.int32, (CHUNK, 1), 0) + r0) < n_valid
                beta_all = jnp.where(valid, beta_all, 0.0)
                g_all = jnp.where(valid, g_all, 0.0)
            for h in heads:
                rows.append(rw)
                q.append(qkv_ref[rw, hs[h]] * (GDN_HD ** -0.5))
                k.append(qkv_ref[rw, HEAD_W + h * GDN_HD:HEAD_W + (h + 1) * GDN_HD])
                v.append(qkv_ref[rw, 2 * HEAD_W + h * GDN_HD:2 * HEAD_W + (h + 1) * GDN_HD])
                beta.append(beta_all[:, h:h + 1])
                col, row = _chunk_cumsum(g_all[:, GDN_HEADS + h:GDN_HEADS + h + 1], r, c)
                gc_col.append(col)
                gc_last.append(col[CHUNK - 1:CHUNK, :])
                decay.append(jnp.exp(jnp.where(incl, col - row, -jnp.inf)))
                kb.append(k[-1] * beta[-1])
                k16.append(_bf(k[-1]))
        every = range(per_iter * GDN_HEADS)
        kk = [_dot_nt(_bf(kb[i]), k16[i]) for i in every]
        qk = [_dot_nt(_bf(q[i]), k16[i]) for i in every]
        inv = _unit_lower_inverses([jnp.where(strict, kk[i] * decay[i], 0.0) for i in every], r, c)
        egc = [jnp.exp(x) for x in gc_col]
        rhs = [jnp.concatenate([v[i] * beta[i], kb[i] * egc[i]], axis=1) for i in every]
        sol = [_dot(_bf(inv[i]), _bf(rhs[i])) for i in every]
        for ci in range(per_iter):
            ids = [ci * GDN_HEADS + h for h in heads]
            s = [s_ref[h] for h in heads]
            s16 = [_bf(x) for x in s]
            ws = [_dot(_bf(sol[i][:, GDN_HD:]), s16[h]) for h, i in enumerate(ids)]
            qs = [_dot(_bf(q[i] * egc[i]), s16[h]) for h, i in enumerate(ids)]
            v16 = [_bf(sol[i][:, :GDN_HD] - ws[h]) for h, i in enumerate(ids)]
            av = [_dot(_bf(qk[i] * decay[i]), v16[h]) for h, i in enumerate(ids)]
            kv = [_dot_tn(_bf(k[i] * jnp.exp(gc_last[i] - gc_col[i])), v16[h]) for h, i in enumerate(ids)]
            for h, i in enumerate(ids):
                s_ref[h] = s[h] * jnp.exp(gc_last[i]) + kv[h]
                o = qs[h] + av[h]
                o_ref[rows[i], hs[h]] = (_rms(o, gain_ref[...]) * _silu(z_ref[rows[i], hs[h]])).astype(o_ref.dtype)
        return carry

    lax.fori_loop(0, n_chunks // per_iter, body, 0)


def _gdn(conv_out, proj, z_tile, small_tile, a_log, dt_bias, gain, s0, batch, t, n_valid):
    pad = jnp.zeros((LANES - 2 * GDN_HEADS,), F32)
    alog_vec = jnp.concatenate([jnp.zeros((GDN_HEADS,), F32), a_log, pad]).reshape(1, LANES)
    bias_vec = jnp.concatenate([jnp.zeros((GDN_HEADS,), F32), dt_bias, pad]).reshape(1, LANES)
    kern = functools.partial(_gdn_kernel, n_valid=n_valid)
    vec = pl.BlockSpec((1, LANES), lambda b: (0, 0))
    st = pl.BlockSpec((None, GDN_HEADS, GDN_HD, GDN_HD), lambda b: (b, 0, 0, 0))
    return pl.pallas_call(
        kern,
        grid=(batch,),
        in_specs=[
            pl.BlockSpec((t, 3 * HEAD_W), lambda b: (b, 0)),
            pl.BlockSpec((t, HEAD_W), lambda b: (b, z_tile)),
            pl.BlockSpec((t, LANES), lambda b: (b, small_tile * (HEAD_W // LANES))),
            vec, vec, vec, st,
        ],
        out_specs=[pl.BlockSpec((t, HEAD_W), lambda b: (b, 0)), st],
        out_shape=[
            jax.ShapeDtypeStruct((batch * t, HEAD_W), BF16),
            jax.ShapeDtypeStruct((batch, GDN_HEADS, GDN_HD, GDN_HD), F32),
        ],
        compiler_params=_cparams("parallel"),
        name="gdn",
    )(conv_out, proj, proj, alog_vec, bias_vec, gain.reshape(1, LANES), s0)


_DT_LANE = 2 * GDN_HEADS


def _ssd_kernel(x_ref, bc_ref, z_ref, sm_ref, alog_ref, bias_ref, d_ref, gain_ref, h0_ref,
                y_ref, h_ref, *, n_valid):
    t = x_ref.shape[0]
    h_ref[...] = h0_ref[...]
    r, c = _chunk_iotas()
    incl = r >= c
    neg_a = -jnp.exp(alog_ref[...])
    lane = lax.broadcasted_iota(jnp.int32, (CHUNK, LANES), 1)
    first = lane < SSM_HD
    srow = lax.broadcasted_iota(jnp.int32, (LANES, 1), 0) < SSM_HD
    pairs_per_group = SSM_HEADS // SSM_GROUPS // 2

    def body(n, carry):
        r0 = pl.multiple_of(n * CHUNK, CHUNK)
        rows = pl.ds(r0, CHUNK)
        dt_all = _softplus(sm_ref[rows, :] + bias_ref[...])
        if n_valid < t:
            valid = (lax.broadcasted_iota(jnp.int32, (CHUNK, 1), 0) + r0) < n_valid
            dt_all = jnp.where(valid, dt_all, 0.0)
        dta_all = dt_all * neg_a
        for g in range(SSM_GROUPS):
            bg = _bf(bc_ref[rows, g * SSM_STATE:(g + 1) * SSM_STATE])
            cg = _bf(bc_ref[rows, (SSM_GROUPS + g) * SSM_STATE:(SSM_GROUPS + g + 1) * SSM_STATE])
            cb = _dot_nt(cg, bg)
            ys = []
            for pr in range(pairs_per_group):
                p = g * pairs_per_group + pr
                ps = slice(p * LANES, (p + 1) * LANES)
                xp = x_ref[rows, ps]
                cols = []
                for hh in (2 * p, 2 * p + 1):
                    dt = dt_all[:, _DT_LANE + hh:_DT_LANE + hh + 1]
                    ac_col, ac_row = _chunk_cumsum(dta_all[:, _DT_LANE + hh:_DT_LANE + hh + 1], r, c)
                    lmask = jnp.exp(jnp.where(incl, ac_col - ac_row, -jnp.inf))
                    cols.append((dt, ac_col, lmask))
                (dt_a, ac_a, lm_a), (dt_b, ac_b, lm_b) = cols
                last_a = ac_a[CHUNK - 1:CHUNK, :]
                last_b = ac_b[CHUNK - 1:CHUNK, :]
                xdt = xp * jnp.where(first, dt_a, dt_b)
                y_intra = (_dot(_bf(cb * lm_a), _bf(jnp.where(first, xdt, 0.0)))
                           + _dot(_bf(cb * lm_b), _bf(jnp.where(first, 0.0, xdt))))
                hp = h_ref[p]
                y_inter = _dot_nt(cg, _bf(hp)) * jnp.where(first, jnp.exp(ac_a), jnp.exp(ac_b))
                x_dec = xdt * jnp.where(first, jnp.exp(last_a - ac_a), jnp.exp(last_b - ac_b))
                h_ref[p] = hp * jnp.where(srow, jnp.exp(last_a), jnp.exp(last_b)) + _dot_tn(_bf(x_dec), bg)
                y = y_intra + y_inter + d_ref[:, ps] * xp
                ys.append(y * _silu(z_ref[rows, ps]))
            ms = sum(jnp.sum(y * y, axis=-1, keepdims=True) for y in ys) / (len(ys) * LANES)
            scale = lax.rsqrt(ms + NORM_EPS)
            for pr, y in enumerate(ys):
                ps = slice((g * pairs_per_group + pr) * LANES, (g * pairs_per_group + pr + 1) * LANES)
                y_ref[rows, ps] = (y * scale * gain_ref[:, ps]).astype(y_ref.dtype)
        return carry

    lax.fori_loop(0, t // CHUNK, body, 0)


def _ssd(conv_out, proj, z_tile, small_tile, a_log, dt_bias, d_skip, gain, h0, batch, t, n_valid):
    npair = SSM_HEADS // 2
    pad_l = jnp.zeros((_DT_LANE,), F32)
    pad_r = jnp.zeros((LANES - _DT_LANE - SSM_HEADS,), F32)
    alog_vec = jnp.concatenate([pad_l, a_log, pad_r]).reshape(1, LANES)
    bias_vec = jnp.concatenate([pad_l, dt_bias, pad_r]).reshape(1, LANES)
    d_vec = jnp.repeat(d_skip, SSM_HD).reshape(1, HEAD_W)
    kern = functools.partial(_ssd_kernel, n_valid=n_valid)
    vec = pl.BlockSpec((1, LANES), lambda b: (0, 0))
    wide = pl.BlockSpec((1, HEAD_W), lambda b: (0, 0))
    st = pl.BlockSpec((None, npair, LANES, SSM_STATE), lambda b: (b, 0, 0, 0))
    y, h = pl.pallas_call(
        kern,
        grid=(batch,),
        in_specs=[
            pl.BlockSpec((t, HEAD_W), lambda b: (b, 3)),
            pl.BlockSpec((t, HEAD_W), lambda b: (b, 4)),
            pl.BlockSpec((t, HEAD_W), lambda b: (b, z_tile)),
            pl.BlockSpec((t, LANES), lambda b: (b, small_tile * (HEAD_W // LANES))),
            vec, vec, wide, wide, st,
        ],
        out_specs=[pl.BlockSpec((t, HEAD_W), lambda b: (b, 0)), st],
        out_shape=[
            jax.ShapeDtypeStruct((batch * t, HEAD_W), BF16),
            jax.ShapeDtypeStruct((batch, npair, LANES, SSM_STATE), F32),
        ],
        compiler_params=_cparams("parallel"),
        name="ssd",
    )(conv_out, conv_out, proj, proj, alog_vec, bias_vec, d_vec, gain.reshape(1, HEAD_W),
      h0.reshape(batch, npair, LANES, SSM_STATE))
    return y, h.reshape(batch, SSM_HEADS, SSM_HD, SSM_STATE)


_ROWS = 8


_PAGE_ROWS = PAGE_SIZE * _ROWS
_PAGES_PER_STEP = 8


def _page_scores(q16, page16, valid, scale):
    return jnp.where(valid, _dot_nt(q16, page16) * scale, -jnp.inf)


def _page_specs(layer):
    def spec(k):
        return pl.BlockSpec((None, None, _PAGE_ROWS, LANES),
                            lambda b, j, pt: (pt[b, j * _PAGES_PER_STEP + k], layer, 0, 0))
    return [spec(k) for k in range(_PAGES_PER_STEP)]


def _moba_step_kernel(pt_ref, q_ref, kn_ref, vn_ref, *rest, n_sel, npg):
    pages = rest[:_PAGES_PER_STEP]
    o_ref, m_ref, l_ref, g_ref, acc_ref = rest[_PAGES_PER_STEP:]
    j = pl.program_id(1)
    scale = MOBA_HD ** -0.5
    lane = lax.broadcasted_iota(jnp.int32, (_ROWS, LANES), 1)

    @pl.when(j == 0)
    def _():
        m_ref[...] = jnp.full(m_ref.shape, -jnp.inf, F32)
        l_ref[...] = jnp.zeros_like(l_ref)
        g_ref[...] = jnp.zeros_like(g_ref)

    q = q_ref[...]
    q16 = _bf(q)
    col = lax.broadcasted_iota(jnp.int32, (_ROWS, _PAGE_ROWS), 1)
    sub = lax.broadcasted_iota(jnp.int32, (_ROWS, _PAGE_ROWS), 0)
    valid = (col % _ROWS) == (sub % MOBA_HEADS)
    m_all, l_all, g_all = m_ref[...], l_ref[...], g_ref[...]
    page16 = [_bf(p[...]) for p in pages]
    s = [_page_scores(q16, p16, valid, scale) for p16 in page16]
    m = [jnp.max(x, axis=-1, keepdims=True) for x in s]
    e = [jnp.exp(x - mx) for x, mx in zip(s, m)]
    acc = [_dot(_bf(pltpu.roll(x, MOBA_HEADS, 1)), p16) for x, p16 in zip(e, page16)]
    for k, page_ref in enumerate(pages):
        idx = j * _PAGES_PER_STEP + k
        acc_ref[idx] = acc[k]
        ksum = jnp.sum(page_ref[...].reshape(PAGE_SIZE, _ROWS, LANES), axis=0)
        here = lane == idx
        m_all = jnp.where(here, m[k], m_all)
        l_all = jnp.where(here, jnp.sum(e[k], axis=-1, keepdims=True), l_all)
        g_all = jnp.where(here, jnp.sum(ksum * q, axis=-1, keepdims=True), g_all)
    m_ref[...] = m_all
    l_ref[...] = l_all
    g_ref[...] = g_all

    @pl.when(j == pl.num_programs(1) - 1)
    def _():
        gb = g_all + jnp.where(lane % 2 == 0, pltpu.roll(g_all, LANES - 1, 1), pltpu.roll(g_all, 1, 1))
        gb = gb * (1.0 / MOBA_BLOCK)
        rank = jnp.zeros((_ROWS, LANES), jnp.int32)
        for jp in range(0, npg, 2):
            gcol = gb[:, jp:jp + 1]
            beats = (gcol > gb) | ((gcol == gb) & (jp // 2 < lane // 2))
            rank = rank + beats.astype(jnp.int32)
        sel = (rank < n_sel) & (lane < npg)
        s_own = jnp.sum(q16.astype(F32) * _bf(kn_ref[...]).astype(F32), axis=-1, keepdims=True) * scale
        m_tot = jnp.maximum(jnp.max(jnp.where(sel, m_all, -jnp.inf), axis=-1, keepdims=True), s_own)
        w = jnp.where(sel, jnp.exp(m_all - m_tot), 0.0)
        e_own = jnp.exp(s_own - m_tot)
        l_tot = jnp.sum(w * l_all, axis=-1, keepdims=True) + e_own
        acc = e_own * _bf(vn_ref[...]).astype(F32)
        for jj in range(npg):
            acc = acc + w[:, jj:jj + 1] * acc_ref[jj]
        o_ref[...] = acc / l_tot


def _moba_step(q8, k8, v8, cache, page_table, layer):
    rows, npg = page_table.shape
    assert npg % _PAGES_PER_STEP == 0 and npg <= LANES
    n_blocks = npg * PAGE_SIZE // MOBA_BLOCK + 1
    kern = functools.partial(_moba_step_kernel, n_sel=min(MOBA_TOPK, n_blocks), npg=npg)
    tile = pl.BlockSpec((None, _ROWS, LANES), lambda b, j, pt: (b, 0, 0))
    stat = pltpu.VMEM((_ROWS, LANES), F32)
    return pl.pallas_call(
        kern,
        grid_spec=pltpu.PrefetchScalarGridSpec(
            num_scalar_prefetch=1,
            grid=(rows, npg // _PAGES_PER_STEP),
            in_specs=[tile, tile, tile] + _page_specs(layer),
            out_specs=tile,
            scratch_shapes=[stat, stat, stat, pltpu.VMEM((npg, _ROWS, LANES), F32)],
        ),
        out_shape=jax.ShapeDtypeStruct((rows, _ROWS, LANES), F32),
        compiler_params=_cparams("parallel", "arbitrary"),
        name="moba_step",
    )(page_table, q8, k8, v8, *([cache] * _PAGES_PER_STEP))


def _diff_step_kernel(pt_ref, q_ref, kn_ref, vn_ref, lp_ref, gn_ref, *rest, lam_init):
    pages = rest[:_PAGES_PER_STEP]
    o_ref, m_ref, l_ref, acc_ref = rest[_PAGES_PER_STEP:]
    j = pl.program_id(1)
    scale = DIFF_HD ** -0.5

    @pl.when(j == 0)
    def _():
        m_ref[...] = jnp.full(m_ref.shape, -jnp.inf, F32)
        l_ref[...] = jnp.zeros_like(l_ref)
        acc_ref[...] = jnp.zeros_like(acc_ref)

    q16 = _bf(q_ref[...])
    col = lax.broadcasted_iota(jnp.int32, (_ROWS, _PAGE_ROWS), 1)
    sub = lax.broadcasted_iota(jnp.int32, (_ROWS, _PAGE_ROWS), 0)
    valid = (col % _ROWS) == (sub // 2)
    m_old, l_old = m_ref[...][:, 0:1], l_ref[...][:, 0:1]
    page16 = [_bf(p[...]) for p in pages]
    s = [_page_scores(q16, p16, valid, scale) for p16 in page16]
    m_run = m_old
    for x in s:
        m_run = jnp.maximum(m_run, jnp.max(x, axis=-1, keepdims=True))
    alpha = jnp.exp(m_old - m_run)
    e = [jnp.exp(x - m_run) for x in s]
    l_run = alpha * l_old + sum(jnp.sum(x, axis=-1, keepdims=True) for x in e)
    acc = alpha * acc_ref[...] + sum(_dot(_bf(pltpu.roll(x, DIFF_HEADS, 1)), p16) for x, p16 in zip(e, page16))
    m_ref[...] = jnp.broadcast_to(m_run, (_ROWS, LANES))
    l_ref[...] = jnp.broadcast_to(l_run, (_ROWS, LANES))
    acc_ref[...] = acc

    @pl.when(j == pl.num_programs(1) - 1)
    def _():
        lam = _diff_lambda(lp_ref, lam_init)
        s_new = jnp.sum(q16.astype(F32) * _bf(kn_ref[...]).astype(F32), axis=-1, keepdims=True) * scale
        m_new = jnp.maximum(m_run, s_new)
        alpha = jnp.exp(m_run - m_new)
        e = jnp.exp(s_new - m_new)
        o = (alpha * acc + e * _bf(vn_ref[...]).astype(F32)) / (alpha * l_run + e)
        d = o - lam * pltpu.roll(o, _ROWS - 1, 0)
        o_ref[...] = _rms(d, gn_ref[...]) * (1.0 - lam_init)


def _diff_step(q8, k8, v8, cache, page_table, layer, lam_params, norm_gain, lam_init):
    rows, npg = page_table.shape
    assert npg % _PAGES_PER_STEP == 0
    kern = functools.partial(_diff_step_kernel, lam_init=lam_init)
    tile = pl.BlockSpec((None, _ROWS, LANES), lambda b, j, pt: (b, 0, 0))
    stat = pltpu.VMEM((_ROWS, LANES), F32)
    return pl.pallas_call(
        kern,
        grid_spec=pltpu.PrefetchScalarGridSpec(
            num_scalar_prefetch=1,
            grid=(rows, npg // _PAGES_PER_STEP),
            in_specs=[tile, tile, tile,
                      pl.BlockSpec((4, DIFF_HD), lambda b, j, pt: (0, 0)),
                      pl.BlockSpec((1, LANES), lambda b, j, pt: (0, 0))] + _page_specs(layer),
            out_specs=tile,
            scratch_shapes=[stat, stat, stat],
        ),
        out_shape=jax.ShapeDtypeStruct((rows, _ROWS, LANES), F32),
        compiler_params=_cparams("parallel", "arbitrary"),
        name="diff_step",
    )(page_table, q8, k8, v8, lam_params, norm_gain.reshape(1, LANES), *([cache] * _PAGES_PER_STEP))


def _rope_tables(pos, hd, reps):
    rd = hd // 4
    half = rd // 2
    inv_freq = 1.0 / (ROPE_THETA ** (jnp.arange(half, dtype=F32) * (2.0 / rd)))
    ang = pos.astype(F32)[:, None] * inv_freq[None, :]
    cos = jnp.cos(ang)
    sin = jnp.sin(ang)
    n = pos.shape[0]
    zero = jnp.zeros((n, half), F32)
    c = jnp.concatenate([cos, cos, jnp.ones((n, hd - rd), F32)], axis=1)
    s1 = jnp.concatenate([zero, sin, jnp.zeros((n, hd - rd), F32)], axis=1)
    s2 = jnp.concatenate([-sin, zero, jnp.zeros((n, hd - rd), F32)], axis=1)
    return tuple(jnp.tile(a, (1, reps)) for a in (c, s1, s2))


def _pad_chunk(a):
    rows, n = a.shape
    return jnp.pad(a[:, None, :], ((0, 0), (0, CHUNK - 1), (0, 0))).reshape(rows * CHUNK, n)


def kernel(x_prompt, x_sample, cache_moba_kv, cache_diff_kv, state_gdn, state_ssm, state_conv, page_table, ffn1_norm, ffn1_w_gu, ffn1_w_down, mix_norm, w_in, conv_w, conv_b, diff_lambda, diff_norm, gdn_A_log, gdn_dt_bias, gdn_norm, ssm_A_log, ssm_dt_bias, ssm_D, ssm_norm, w_branch, w_out, ffn2_norm, ffn2_w_gu, ffn2_w_down, final_norm):
    bp, t, d = x_prompt.shape
    bs = x_sample.shape[0]
    depth = w_in.shape[0]
    assert d == D_MODEL and x_sample.shape[1] == 1 and t % MOBA_BLOCK == 0
    past_len = page_table.shape[1] * PAGE_SIZE
    assert past_len % MOBA_BLOCK == 0
    n_phys = cache_moba_kv.shape[0]
    cache_m = cache_moba_kv.reshape(n_phys, depth, _PAGE_ROWS, LANES)
    cache_d = cache_diff_kv.reshape(n_phys, depth, _PAGE_ROWS, LANES)
    comp_mask = (jnp.arange(LANES)[None, :] // DIFF_HD == jnp.arange(_ROWS)[:, None] % 2).astype(F32)

    tm_p = 512 if (bp * t) % 512 == 0 else MOBA_BLOCK
    tm_proj = 1024 if t % 1024 == 0 else tm_p
    pos_p = jnp.arange(t, dtype=jnp.int32)
    pos_s = jnp.full((bs,), past_len, jnp.int32)
    tables_p = _rope_tables(pos_p, MOBA_HD, 1) + _rope_tables(pos_p, DIFF_HD, 2)
    tables_s = _rope_tables(pos_s, MOBA_HD, 1) + _rope_tables(pos_s, DIFF_HD, 2)

    hp = x_prompt.reshape(bp * t, d)
    hs = x_sample.reshape(bs, d)
    outs = {k: [] for k in ("moba_p", "moba_s", "diff_p", "diff_s", "gdn_p", "gdn_s", "ssm_p", "ssm_s",
                            "conv_p", "conv_s")}
    y_p = y_s = None
    for l in range(depth):
        last = l == depth - 1
        lam_init = 0.8 - 0.6 * math.exp(-0.3 * l)
        wi = w_in[l]
        w_proj = jnp.concatenate([wi[:, _C_MOBA:_C_GDNB], wi[:, _C_SSMZ:_C_SSMDT], wi[:, _C_GDNB:_C_SSMZ],
                                  wi[:, _C_SSMDT:_C_GATE], jnp.zeros((d, HEAD_W - 4 * GDN_HEADS), F32)],
                                 axis=1).astype(BF16)
        w_gate = jnp.transpose(wi[:, _C_GATE:].reshape(d, N_BRANCH, d), (1, 0, 2)).astype(BF16)
        w_br = w_branch[l].astype(BF16)
        w_o = w_out[l].astype(BF16)
        wgu1 = ffn1_w_gu[l].astype(BF16)
        wd1 = ffn1_w_down[l].astype(BF16)
        wgu2 = ffn2_w_gu[l].astype(BF16)
        wd2 = ffn2_w_down[l].astype(BF16)
        post_gain = final_norm if last else ffn1_norm[l + 1]
        post_dtype = F32 if last else BF16

        def dense_in(h, tm, tm_proj, tables, rows_per_seq):
            h1, u = _ffn(h, ffn1_norm[l], wgu1, wd1, mix_norm[l], BF16, tm)
            return h1, u, _in_proj(u, w_proj, tables, tm_proj, rows_per_seq)

        def dense_out(h1, u, branches, tm):
            merged = _merge(u, branches, w_gate, w_br, tm)
            h2 = _outproj(merged, w_o, h1, tm)
            return _ffn(h2, ffn2_norm[l], wgu2, wd2, post_gain, post_dtype, tm)

        h1, u, proj = dense_in(hp, tm_p, tm_proj, tables_p, t)
        o_moba = _moba_prompt(proj, bp, t)
        o_diff = _diff_prompt(proj, diff_lambda[l], diff_norm[l], lam_init, bp, t)
        conv_out, conv_new = _conv_prompt(proj, conv_w[l], conv_b[l], bp, t, _P_CONV)
        o_gdn, s_new = _gdn(conv_out, proj, _P_GDNZ, _P_SMALL, gdn_A_log[l], gdn_dt_bias[l], gdn_norm[l],
                            jnp.zeros((bp, GDN_HEADS, GDN_HD, GDN_HD), F32), bp, t, t)
        y_ssm, h_new = _ssd(conv_out, proj, _P_SSMZ, _P_SMALL, ssm_A_log[l], ssm_dt_bias[l], ssm_D[l],
                            ssm_norm[l], jnp.zeros((bp, SSM_HEADS, SSM_HD, SSM_STATE), F32), bp, t, t)
        hp, y_p = dense_out(h1, u, (o_moba, o_diff, o_gdn, y_ssm), tm_p)
        outs["moba_p"].append(proj[:, HEAD_W:3 * HEAD_W].reshape(bp, t, 2, MOBA_HEADS, MOBA_HD))
        outs["diff_p"].append(proj[:, _C_DIFF + HEAD_W:_C_CONV].reshape(bp, t, 2, DIFF_HEADS, 2 * DIFF_HD))
        outs["gdn_p"].append(s_new)
        outs["ssm_p"].append(h_new)
        outs["conv_p"].append(conv_new)

        h1, u, proj = dense_in(hs, bs, bs, tables_s, 1)
        qkv = proj[:, :_C_CONV]
        heads = qkv.reshape(bs, 6, 4, LANES)
        mq, mk, mv = (jnp.tile(heads[:, i], (1, 2, 1)) for i in range(3))
        dq, dk, dv = (jnp.repeat(heads[:, i], 2, axis=1) for i in range(3, 6))
        o_moba = _moba_step(mq, mk, mv, cache_m, page_table, l)[:, :MOBA_HEADS].reshape(bs, HEAD_W).astype(BF16)
        o_diff = _diff_step(dq * comp_mask, dk, dv, cache_d, page_table, l, diff_lambda[l], diff_norm[l], lam_init)
        o_diff = o_diff[:, ::2].reshape(bs, HEAD_W).astype(BF16)
        conv_out, buf_new = _conv_step(proj[:, _C_CONV:_P_GDNZ * HEAD_W], jnp.moveaxis(state_conv[l], 1, 0),
                                       conv_w[l], conv_b[l])
        conv_pad = _pad_chunk(conv_out)
        tail_pad = _pad_chunk(proj[:, _P_GDNZ * HEAD_W:])
        o_gdn, s_new = _gdn(conv_pad, tail_pad, 0, 2, gdn_A_log[l], gdn_dt_bias[l], gdn_norm[l],
                            state_gdn[l], bs, CHUNK, 1)
        y_ssm, h_new = _ssd(conv_pad, tail_pad, 1, 2, ssm_A_log[l], ssm_dt_bias[l], ssm_D[l], ssm_norm[l],
                            state_ssm[l], bs, CHUNK, 1)
        o_gdn = o_gdn.reshape(bs, CHUNK, HEAD_W)[:, 0]
        y_ssm = y_ssm.reshape(bs, CHUNK, HEAD_W)[:, 0]
        hs, y_s = dense_out(h1, u, (o_moba, o_diff, o_gdn, y_ssm), bs)
        outs["moba_s"].append(qkv[:, HEAD_W:3 * HEAD_W].reshape(bs, 1, 2, MOBA_HEADS, MOBA_HD))
        outs["diff_s"].append(qkv[:, _C_DIFF + HEAD_W:].reshape(bs, 1, 2, DIFF_HEADS, 2 * DIFF_HD))
        outs["gdn_s"].append(s_new)
        outs["ssm_s"].append(h_new)
        outs["conv_s"].append(jnp.moveaxis(buf_new, 0, 1))

    return (y_p.reshape(bp, t, d), y_s.reshape(bs, 1, d),
            jnp.stack(outs["moba_p"], axis=1), jnp.stack(outs["moba_s"], axis=1),
            jnp.stack(outs["diff_p"], axis=1), jnp.stack(outs["diff_s"], axis=1),
            jnp.stack(outs["gdn_p"]), jnp.stack(outs["gdn_s"]),
            jnp.stack(outs["ssm_p"]), jnp.stack(outs["ssm_s"]),
            jnp.stack(outs["conv_p"]), jnp.stack(outs["conv_s"]))
```

```python
import functools
import math

import jax
import jax.numpy as jnp
from jax import lax
from jax.experimental import pallas as pl
from jax.experimental.pallas import tpu as pltpu

F32 = jnp.float32
BF16 = jnp.bfloat16

D_MODEL = 2048
D_FF = 5632
NORM_EPS = 1e-6
ROPE_THETA = 500000.0
PAGE_SIZE = 128

MOBA_HEADS = 4
MOBA_HD = 128
MOBA_BLOCK = 256
MOBA_TOPK = 3
DIFF_HEADS = 4
DIFF_HD = 64
GDN_HEADS = 4
GDN_HD = 128
SSM_HEADS = 8
SSM_HD = 64
SSM_GROUPS = 2
SSM_STATE = 128
CHUNK = 64
CONV_W = 4
HEAD_W = 512
CONV_CH = 2560
N_BRANCH = 4

LANES = 128
V7X_VMEM_LIMIT = 56 * 1024 * 1024

_C_MOBA = 0
_C_DIFF = 1536
_C_CONV = 3072
_C_GDNZ = 5632
_C_GDNB = 6144
_C_GDNA = 6148
_C_SSMZ = 6152
_C_SSMDT = 6664
_C_GATE = 6672


def _cparams(*sem):
    return pltpu.CompilerParams(dimension_semantics=sem, vmem_limit_bytes=V7X_VMEM_LIMIT)


def _rms(x, gain):
    y = x * lax.rsqrt(jnp.mean(x * x, axis=-1, keepdims=True) + NORM_EPS)
    return y * gain


def _dot(a, b):
    return jnp.dot(a, b, preferred_element_type=F32)


def _dot_nt(a, b):
    return lax.dot_general(a, b, (((1,), (1,)), ((), ())), preferred_element_type=F32)


def _dot_tn(a, b):
    return lax.dot_general(a, b, (((0,), (0,)), ((), ())), preferred_element_type=F32)


def _softplus(x):
    return jnp.maximum(x, 0.0) + jnp.log(1.0 + jnp.exp(-jnp.abs(x)))


def _sigmoid(x):
    return 1.0 / (1.0 + jnp.exp(-x))


def _silu(x):
    return x * _sigmoid(x)


_FFN_SLICES = 2


def _ffn_kernel(x_ref, g_ref, wg_ref, wu_ref, wd_ref, pg_ref, o_ref, p_ref, xn_ref):
    f = pl.program_id(1)

    @pl.when(f == 0)
    def _():
        x = x_ref[...]
        xn_ref[...] = _rms(x, g_ref[...]).astype(BF16)
        o_ref[...] = x

    xn = xn_ref[...]
    tf = wg_ref.shape[1]
    w = tf // _FFN_SLICES
    gu = [(_dot(xn, wg_ref[:, c * w:(c + 1) * w]), _dot(xn, wu_ref[:, c * w:(c + 1) * w]))
          for c in range(_FFN_SLICES)]
    acc = None
    for c, (gate, up) in enumerate(gu):
        act = (0.5 * _silu(gate) * up).astype(BF16)
        part = _dot(act, wd_ref[c * w:(c + 1) * w, :])
        acc = part if acc is None else acc + part
    o_ref[...] += acc

    @pl.when(f == pl.num_programs(1) - 1)
    def _():
        p_ref[...] = _rms(o_ref[...], pg_ref[...]).astype(p_ref.dtype)


def _ffn(x, gain, w_gu, w_down, post_gain, post_dtype, tm, tf=512):
    m = x.shape[0]
    nf = D_FF // tf
    return pl.pallas_call(
        _ffn_kernel,
        grid=(m // tm, nf),
        in_specs=[
            pl.BlockSpec((tm, D_MODEL), lambda i, f: (i, 0)),
            pl.BlockSpec((1, D_MODEL), lambda i, f: (0, 0)),
            pl.BlockSpec((D_MODEL, tf), lambda i, f: (0, f)),
            pl.BlockSpec((D_MODEL, tf), lambda i, f: (0, f + nf)),
            pl.BlockSpec((tf, D_MODEL), lambda i, f: (f, 0)),
            pl.BlockSpec((1, D_MODEL), lambda i, f: (0, 0)),
        ],
        out_specs=[
            pl.BlockSpec((tm, D_MODEL), lambda i, f: (i, 0)),
            pl.BlockSpec((tm, D_MODEL), lambda i, f: (i, 0)),
        ],
        out_shape=[
            jax.ShapeDtypeStruct((m, D_MODEL), F32),
            jax.ShapeDtypeStruct((m, D_MODEL), post_dtype),
        ],
        scratch_shapes=[pltpu.VMEM((tm, D_MODEL), BF16)],
        compiler_params=_cparams("parallel", "arbitrary"),
        name="ffn",
    )(x, gain.reshape(1, D_MODEL), w_gu, w_gu, w_down, post_gain.reshape(1, D_MODEL))


def _rope(x, c_ref, s1_ref, s2_ref, half):
    n = x.shape[1]
    reps = n // LANES
    c = jnp.concatenate([c_ref[...]] * reps, axis=1)
    s1 = jnp.concatenate([s1_ref[...]] * reps, axis=1)
    s2 = jnp.concatenate([s2_ref[...]] * reps, axis=1)
    return x * c + pltpu.roll(x, half, 1) * s1 + pltpu.roll(x, n - half, 1) * s2


_P_CONV = 6
_P_GDNZ = 11
_P_SSMZ = 12
_P_SMALL = 13
_P_TILES = 14


_KV_ROWS = 8


def _in_proj_kernel(u_ref, w_ref, mc_ref, ms1_ref, ms2_ref, dc_ref, ds1_ref, ds2_ref, *rest, kv_rows):
    if kv_rows:
        o_ref, kvm_ref, kvd_ref = rest[-3:]
    else:
        (o_ref,) = rest[-1:]
        kvm_ref = kvd_ref = None
    j = pl.program_id(1)
    w = w_ref.shape[1] // 2
    tm = u_ref.shape[0]

    def rope_m(acc):
        return _rope(acc, mc_ref, ms1_ref, ms2_ref, MOBA_HD // 8)

    def rope_d(acc):
        return _rope(acc, dc_ref, ds1_ref, ds2_ref, DIFF_HD // 8)

    def plain(acc):
        return acc

    def tile(epilogue, kv_ref=None, kv=0):
        u = u_ref[...]
        halves = [_dot(u, w_ref[:, c * w:(c + 1) * w]) for c in range(2)]
        for c, acc in enumerate(halves):
            val = epilogue(acc)
            o_ref[:, c * w:(c + 1) * w] = val
            if kv_ref is not None:
                for hh in range(w // LANES):
                    row0 = kv * (_KV_ROWS // 2) + c * (w // LANES) + hh
                    kv_ref[pl.ds(row0, tm, stride=_KV_ROWS), :] = val[:, hh * LANES:(hh + 1) * LANES]

    cases = [(rope_m, None, 0), (rope_m, kvm_ref, 0), (plain, kvm_ref, 1),
             (rope_d, None, 0), (rope_d, kvd_ref, 0), (plain, kvd_ref, 1)]
    for jj, (epilogue, kv_ref, kv) in enumerate(cases):
        pl.when(j == jj)(functools.partial(tile, epilogue, kv_ref, kv))
    pl.when(j >= len(cases))(functools.partial(tile, plain))


def _in_proj(u, w, tables, tm, rows_per_seq, layer=None, depth=None, prev_kv=None):
    m = u.shape[0]
    tn = HEAD_W
    nt = rows_per_seq // tm if rows_per_seq >= tm else 1
    tspec = pl.BlockSpec((tm, LANES), lambda i, j: (i % nt, 0))
    in_specs = [
        pl.BlockSpec((tm, D_MODEL), lambda i, j: (i, 0)),
        pl.BlockSpec((D_MODEL, tn), lambda i, j: (0, j)),
    ] + [tspec] * 6
    out_specs = [pl.BlockSpec((tm, tn), lambda i, j: (i, j))]
    out_shape = [jax.ShapeDtypeStruct((m, _P_TILES * tn), F32)]
    args = [u, w, *tables]
    aliases = {}
    if layer is not None:
        kv_spec = pl.BlockSpec((None, None, tm * _KV_ROWS, LANES), lambda i, j: (i // nt, layer, i % nt, 0))
        kv_shape = jax.ShapeDtypeStruct((m // rows_per_seq, depth, rows_per_seq * _KV_ROWS, LANES), F32)
        out_specs += [kv_spec, kv_spec]
        out_shape += [kv_shape, kv_shape]
        if prev_kv is not None:
            aliases = {len(args): 1, len(args) + 1: 2}
            in_specs += [pl.BlockSpec(memory_space=pl.ANY)] * 2
            args += list(prev_kv)
    res = pl.pallas_call(
        functools.partial(_in_proj_kernel, kv_rows=layer is not None),
        grid=(m // tm, _P_TILES),
        in_specs=in_specs,
        out_specs=out_specs,
        out_shape=out_shape,
        input_output_aliases=aliases,
        compiler_params=_cparams("parallel", "arbitrary"),
        name="in_proj",
    )(*args)
    return res[0] if layer is None else res


def _merge_kernel(u_ref, b0_ref, b1_ref, b2_ref, b3_ref, wg_ref, wb_ref, o_ref):
    u = u_ref[...]
    acc = None
    for i, b_ref in enumerate((b0_ref, b1_ref, b2_ref, b3_ref)):
        gate = _sigmoid(_dot(u, wg_ref[i]))
        term = gate * _dot(b_ref[...], wb_ref[i])
        acc = term if acc is None else acc + term
    o_ref[...] = acc.astype(o_ref.dtype)


def _merge(u, branches, w_gate, w_branch, tm, tn=256):
    m = u.shape[0]
    bspec = pl.BlockSpec((tm, HEAD_W), lambda i, j: (i, 0))
    return pl.pallas_call(
        _merge_kernel,
        grid=(m // tm, D_MODEL // tn),
        in_specs=[pl.BlockSpec((tm, D_MODEL), lambda i, j: (i, 0)), bspec, bspec, bspec, bspec,
                  pl.BlockSpec((N_BRANCH, D_MODEL, tn), lambda i, j: (0, 0, j)),
                  pl.BlockSpec((N_BRANCH, HEAD_W, tn), lambda i, j: (0, 0, j))],
        out_specs=pl.BlockSpec((tm, tn), lambda i, j: (i, j)),
        out_shape=jax.ShapeDtypeStruct((m, D_MODEL), BF16),
        compiler_params=_cparams("parallel", "arbitrary"),
        name="merge",
    )(u, *branches, w_gate, w_branch)


def _outproj_kernel(a_ref, w_ref, h_ref, o_ref):
    o_ref[...] = h_ref[...] + _dot(a_ref[...], w_ref[...])


def _outproj(a, w, h, tm, tn=512):
    m = a.shape[0]
    return pl.pallas_call(
        _outproj_kernel,
        grid=(m // tm, D_MODEL // tn),
        in_specs=[pl.BlockSpec((tm, D_MODEL), lambda i, j: (i, 0)),
                  pl.BlockSpec((D_MODEL, tn), lambda i, j: (0, j)),
                  pl.BlockSpec((tm, tn), lambda i, j: (i, j))],
        out_specs=pl.BlockSpec((tm, tn), lambda i, j: (i, j)),
        out_shape=jax.ShapeDtypeStruct((m, D_MODEL), F32),
        compiler_params=_cparams("parallel", "arbitrary"),
        name="outproj",
    )(a, w, h)


def _split_hi_lo(x):
    hi = x.astype(BF16)
    lo = (x - hi.astype(F32)).astype(BF16)
    return hi, lo


def _moba_kernel(q_ref, k_ref, v_ref, o_ref, kb_ref, vb_ref, km_ref, *, nb, n_sel):
    qi = pl.program_id(2)
    t = k_ref.shape[0]
    bq = q_ref.shape[0]

    @pl.when(qi == 0)
    def _():
        k = k_ref[...]
        kb_ref[...] = k.astype(BF16)
        vb_ref[...] = v_ref[...].astype(BF16)
        km_ref[...] = jnp.zeros_like(km_ref)
        for n in range(nb):
            km_ref[n:n + 1, :] = jnp.mean(k[n * MOBA_BLOCK:(n + 1) * MOBA_BLOCK, :], axis=0, keepdims=True)

    q = q_ref[...]
    qh, ql = _split_hi_lo(q)
    kh, kl = _split_hi_lo(km_ref[...])
    gate = _dot_nt(qh, kh) + (_dot_nt(qh, kl) + _dot_nt(ql, kh))
    lane = lax.broadcasted_iota(jnp.int32, gate.shape, 1)
    past = lane < qi
    gate = jnp.where(past, gate, -jnp.inf)
    rank = jnp.zeros(gate.shape, jnp.int32)
    for m in range(nb):
        gm = gate[:, m:m + 1]
        beats = (gm > gate) | ((gm == gate) & (m < lane))
        rank = rank + beats.astype(jnp.int32)
    sel = (past & (rank < n_sel)).astype(F32).astype(BF16)
    q16 = q.astype(BF16)

    scale = MOBA_HD ** -0.5
    lrow = lax.broadcasted_iota(jnp.int32, (bq, bq), 0)
    lcol = lax.broadcasted_iota(jnp.int32, (bq, bq), 1)

    def attend(tk):
        tp = tk - bq
        s_own = jnp.where(lcol <= lrow, _dot_nt(q16, kb_ref[tp:tk, :]) * scale, -jnp.inf)
        m = jnp.max(s_own, axis=-1, keepdims=True)
        if tp:
            erow = lax.broadcasted_iota(jnp.int32, (LANES, tp), 0)
            ecol = lax.broadcasted_iota(jnp.int32, (LANES, tp), 1)
            expand = (ecol // MOBA_BLOCK == erow).astype(F32).astype(BF16)
            s_past = jnp.where(_dot(sel, expand) > 0.5, _dot_nt(q16, kb_ref[0:tp, :]) * scale, -jnp.inf)
            m = jnp.maximum(m, jnp.max(s_past, axis=-1, keepdims=True))
        e_own = jnp.exp(s_own - m)
        l = jnp.sum(e_own, axis=-1, keepdims=True)
        acc = _dot(e_own.astype(BF16), vb_ref[tp:tk, :])
        if tp:
            e_past = jnp.exp(s_past - m)
            l = l + jnp.sum(e_past, axis=-1, keepdims=True)
            acc = acc + _dot(e_past.astype(BF16), vb_ref[0:tp, :])
        o_ref[...] = (acc / l).astype(o_ref.dtype)

    for n in range(t // bq):
        pl.when(qi == n)(functools.partial(attend, (n + 1) * bq))


def _moba_prompt(qkv, batch, t):
    bq = MOBA_BLOCK
    nq = t // bq
    nb = t // MOBA_BLOCK
    kern = functools.partial(_moba_kernel, nb=nb, n_sel=min(MOBA_TOPK, nb))
    return pl.pallas_call(
        kern,
        grid=(batch, MOBA_HEADS, nq),
        in_specs=[
            pl.BlockSpec((bq, MOBA_HD), lambda b, h, i: (b * nq + i, h)),
            pl.BlockSpec((t, MOBA_HD), lambda b, h, i: (b, MOBA_HEADS + h)),
            pl.BlockSpec((t, MOBA_HD), lambda b, h, i: (b, 2 * MOBA_HEADS + h)),
        ],
        out_specs=pl.BlockSpec((bq, MOBA_HD), lambda b, h, i: (b * nq + i, h)),
        out_shape=jax.ShapeDtypeStruct((batch * t, HEAD_W), BF16),
        scratch_shapes=[pltpu.VMEM((t, MOBA_HD), BF16), pltpu.VMEM((t, MOBA_HD), BF16),
                        pltpu.VMEM((LANES, MOBA_HD), F32)],
        compiler_params=_cparams("parallel", "parallel", "arbitrary"),
        name="moba_prompt",
    )(qkv, qkv, qkv)


def _diff_lambda(lp_ref, lam_init):
    lp = lp_ref[...]
    a = jnp.sum(lp[0:1] * lp[1:2], axis=-1, keepdims=True)
    b = jnp.sum(lp[2:3] * lp[3:4], axis=-1, keepdims=True)
    return jnp.exp(a) - jnp.exp(b) + lam_init


def _diff_kernel(q_ref, k_ref, v_ref, lp_ref, g_ref, o_ref, kb_ref, vb_ref, *, lam_init):
    qi = pl.program_id(2)
    t = k_ref.shape[0]
    bq = q_ref.shape[0]

    @pl.when(qi == 0)
    def _():
        kb_ref[...] = k_ref[...].astype(BF16)
        vb_ref[...] = v_ref[...].astype(BF16)

    lam = _diff_lambda(lp_ref, lam_init)
    q = q_ref[...]
    lane = lax.broadcasted_iota(jnp.int32, q.shape, 1)
    q = q * (DIFF_HD ** -0.5)
    q1 = jnp.where(lane < DIFF_HD, q, 0.0).astype(BF16)
    q2 = jnp.where(lane >= DIFF_HD, q, 0.0).astype(BF16)
    lrow = lax.broadcasted_iota(jnp.int32, (bq, bq), 0)
    lcol = lax.broadcasted_iota(jnp.int32, (bq, bq), 1)

    def attend(tk):
        tp = tk - bq
        qs = (q1, q2)
        s_own = [jnp.where(lcol <= lrow, _dot_nt(qc, kb_ref[tp:tk, :]), -jnp.inf) for qc in qs]
        m = [jnp.max(s, axis=-1, keepdims=True) for s in s_own]
        if tp:
            s_past = [_dot_nt(qc, kb_ref[0:tp, :]) for qc in qs]
            m = [jnp.maximum(mx, jnp.max(s, axis=-1, keepdims=True)) for mx, s in zip(m, s_past)]
        e_own = [jnp.exp(s - mx) for s, mx in zip(s_own, m)]
        l = [jnp.sum(e, axis=-1, keepdims=True) for e in e_own]
        acc = [_dot(e.astype(BF16), vb_ref[tp:tk, :]) for e in e_own]
        if tp:
            e_past = [jnp.exp(s - mx) for s, mx in zip(s_past, m)]
            l = [x + jnp.sum(e, axis=-1, keepdims=True) for x, e in zip(l, e_past)]
            acc = [a + _dot(e.astype(BF16), vb_ref[0:tp, :]) for a, e in zip(acc, e_past)]
        o = acc[0] / l[0] - lam * (acc[1] / l[1])
        o_ref[...] = (_rms(o, g_ref[...]) * (1.0 - lam_init)).astype(o_ref.dtype)

    for n in range(t // bq):
        pl.when(qi == n)(functools.partial(attend, (n + 1) * bq))


def _diff_prompt(qkv, lam_params, norm_gain, lam_init, batch, t):
    bq = 512 if t % 512 == 0 else 256
    nq = t // bq
    c0 = _C_DIFF // LANES
    kern = functools.partial(_diff_kernel, lam_init=lam_init)
    return pl.pallas_call(
        kern,
        grid=(batch, DIFF_HEADS, nq),
        in_specs=[
            pl.BlockSpec((bq, LANES), lambda b, h, i: (b * nq + i, c0 + h)),
            pl.BlockSpec((t, LANES), lambda b, h, i: (b, c0 + DIFF_HEADS + h)),
            pl.BlockSpec((t, LANES), lambda b, h, i: (b, c0 + 2 * DIFF_HEADS + h)),
            pl.BlockSpec((4, DIFF_HD), lambda b, h, i: (0, 0)),
            pl.BlockSpec((1, LANES), lambda b, h, i: (0, 0)),
        ],
        out_specs=pl.BlockSpec((bq, LANES), lambda b, h, i: (b * nq + i, h)),
        out_shape=jax.ShapeDtypeStruct((batch * t, HEAD_W), BF16),
        scratch_shapes=[pltpu.VMEM((t, LANES), BF16), pltpu.VMEM((t, LANES), BF16)],
        compiler_params=_cparams("parallel", "parallel", "arbitrary"),
        name="diff_prompt",
    )(qkv, qkv, qkv, lam_params, norm_gain.reshape(1, LANES))


def _l2norm_heads(x):
    parts = []
    for h in range(x.shape[1] // LANES):
        xh = x[:, h * LANES:(h + 1) * LANES]
        parts.append(xh * lax.rsqrt(jnp.sum(xh * xh, axis=-1, keepdims=True) + 1e-6))
    return jnp.concatenate(parts, axis=1)


def _conv_kernel(x_ref, w_ref, b_ref, o_ref, tail_ref):
    c = pl.program_id(1)
    x = x_ref[...]
    t = x.shape[0]
    row = lax.broadcasted_iota(jnp.int32, x.shape, 0)
    w = w_ref[...]
    acc = b_ref[...] + w[CONV_W - 1:CONV_W] * x
    for d in range(1, CONV_W):
        shifted = jnp.where(row >= d, pltpu.roll(x, d, 0), 0.0)
        acc = acc + w[CONV_W - 1 - d:CONV_W - d] * shifted
    y = _silu(acc)
    tail_ref[...] = x[t - (CONV_W - 1):, :]

    @pl.when(c < 2)
    def _():
        o_ref[...] = _l2norm_heads(y)

    @pl.when(c >= 2)
    def _():
        o_ref[...] = y


def _conv_prompt(x, w, b, batch, t, tile0):
    tc = HEAD_W
    return pl.pallas_call(
        _conv_kernel,
        grid=(batch, CONV_CH // tc),
        in_specs=[
            pl.BlockSpec((t, tc), lambda bi, c: (bi, tile0 + c)),
            pl.BlockSpec((CONV_W, tc), lambda bi, c: (0, c)),
            pl.BlockSpec((1, tc), lambda bi, c: (0, c)),
        ],
        out_specs=[
            pl.BlockSpec((t, tc), lambda bi, c: (bi, c)),
            pl.BlockSpec((None, CONV_W - 1, tc), lambda bi, c: (bi, 0, c)),
        ],
        out_shape=[
            jax.ShapeDtypeStruct((batch * t, CONV_CH), F32),
            jax.ShapeDtypeStruct((batch, CONV_W - 1, CONV_CH), F32),
        ],
        compiler_params=_cparams("parallel", "parallel"),
        name="conv_prompt",
    )(x, w, b.reshape(1, CONV_CH))


def _conv_step_kernel(x_ref, buf_ref, w_ref, b_ref, o_ref, nb_ref):
    x = x_ref[...]
    w = w_ref[...]
    acc = b_ref[...] + w[CONV_W - 1:CONV_W] * x
    for j in range(CONV_W - 1):
        acc = acc + w[j:j + 1] * buf_ref[j]
    y = _silu(acc)
    qk = _l2norm_heads(y[:, :2 * HEAD_W])
    o_ref[...] = jnp.concatenate([qk, y[:, 2 * HEAD_W:]], axis=1)
    for j in range(CONV_W - 2):
        nb_ref[j] = buf_ref[j + 1]
    nb_ref[CONV_W - 2] = x


def _conv_step(x, buf, w, b):
    rows = x.shape[0]
    return pl.pallas_call(
        _conv_step_kernel,
        out_shape=[
            jax.ShapeDtypeStruct((rows, CONV_CH), F32),
            jax.ShapeDtypeStruct((CONV_W - 1, rows, CONV_CH), F32),
        ],
        compiler_params=pltpu.CompilerParams(vmem_limit_bytes=V7X_VMEM_LIMIT),
        name="conv_step",
    )(x, buf, w, b.reshape(1, CONV_CH))


def _chunk_iotas():
    r = lax.broadcasted_iota(jnp.int32, (CHUNK, CHUNK), 0)
    c = lax.broadcasted_iota(jnp.int32, (CHUNK, CHUNK), 1)
    return r, c


def _chunk_cumsum(col, r, c):
    row = jnp.sum(jnp.where(r == c, col, 0.0), axis=0, keepdims=True)
    cs_col = jnp.sum(jnp.where(c <= r, row, 0.0), axis=1, keepdims=True)
    cs_row = jnp.sum(jnp.where(r <= c, col, 0.0), axis=0, keepdims=True)
    return cs_col, cs_row


def _mm3(a, b):
    ah, al = a
    bh, bl = b
    return _dot(ah, bh) + (_dot(ah, bl) + _dot(al, bh))


_INV_BASE = 8


def _unit_lower_inverses(lows, r, c):
    def same_block(n):
        return (r // n) == (c // n)

    eye = (r == c).astype(F32)
    d1 = [jnp.where(same_block(_INV_BASE), low, 0.0) for low in lows]
    s1 = [_split_hi_lo(x) for x in d1]
    d2 = [_mm3(a, a) for a in s1]
    s2 = [_split_hi_lo(x) for x in d2]
    d3 = [_mm3(a, b) for a, b in zip(s1, s2)]
    d4 = [_mm3(b, b) for b in s2]
    inv = [eye - a + b - x for a, b, x in zip(d1, d2, d3)]
    inv = [x + _mm3(_split_hi_lo(x), _split_hi_lo(y)) for x, y in zip(inv, d4)]
    n = _INV_BASE
    while n < CHUNK:
        join = same_block(2 * n) & jnp.logical_not(same_block(n))
        si = [_split_hi_lo(x) for x in inv]
        mid = [_mm3(a, _split_hi_lo(jnp.where(join, low, 0.0))) for a, low in zip(si, lows)]
        inv = [x - _mm3(_split_hi_lo(y), a) for x, y, a in zip(inv, mid, si)]
        n *= 2
    return inv


def _bf(x):
    return x.astype(BF16)


def _gdn_kernel(qkv_ref, z_ref, sm_ref, alog_ref, bias_ref, gain_ref, s0_ref, o_ref, s_ref, *, n_valid):
    t = qkv_ref.shape[0]
    s_ref[...] = s0_ref[...]
    r, c = _chunk_iotas()
    incl = r >= c
    strict = r > c
    neg_a = -jnp.exp(alog_ref[...])

    n_chunks = t // CHUNK
    per_iter = 2 if n_chunks % 2 == 0 else 1
    heads = range(GDN_HEADS)
    hs = [slice(h * GDN_HD, (h + 1) * GDN_HD) for h in heads]

    def body(n, carry):
        rows, q, k, v, beta, gc_col, gc_last, decay, kb, k16 = ([] for _ in range(10))
        for ci in range(per_iter):
            r0 = pl.multiple_of((n * per_iter + ci) * CHUNK, CHUNK)
            rw = pl.ds(r0, CHUNK)
            sm = sm_ref[rw, :]
            beta_all = _sigmoid(sm)
            g_all = neg_a * _softplus(sm + bias_ref[...])
            if n_valid < t:
                valid = (lax.broadcasted_iota(jnp.int32, (CHUNK, 1), 0) + r0) < n_valid
                beta_all = jnp.where(valid, beta_all, 0.0)
                g_all = jnp.where(valid, g_all, 0.0)
            for h in heads:
                rows.append(rw)
                q.append(qkv_ref[rw, hs[h]] * (GDN_HD ** -0.5))
                k.append(qkv_ref[rw, HEAD_W + h * GDN_HD:HEAD_W + (h + 1) * GDN_HD])
                v.append(qkv_ref[rw, 2 * HEAD_W + h * GDN_HD:2 * HEAD_W + (h + 1) * GDN_HD])
                beta.append(beta_all[:, h:h + 1])
                col, row = _chunk_cumsum(g_all[:, GDN_HEADS + h:GDN_HEADS + h + 1], r, c)
                gc_col.append(col)
                gc_last.append(col[CHUNK - 1:CHUNK, :])
                decay.append(jnp.exp(jnp.where(incl, col - row, -jnp.inf)))
                kb.append(k[-1] * beta[-1])
                k16.append(_bf(k[-1]))
        every = range(per_iter * GDN_HEADS)
        kk = [_dot_nt(_bf(kb[i]), k16[i]) for i in every]
        qk = [_dot_nt(_bf(q[i]), k16[i]) for i in every]
        inv = _unit_lower_inverses([jnp.where(strict, kk[i] * decay[i], 0.0) for i in every], r, c)
        egc = [jnp.exp(x) for x in gc_col]
        rhs = [jnp.concatenate([v[i] * beta[i], kb[i] * egc[i]], axis=1) for i in every]
        sol = [_dot(_bf(inv[i]), _bf(rhs[i])) for i in every]
        for ci in range(per_iter):
            ids = [ci * GDN_HEADS + h for h in heads]
            s = [s_ref[h] for h in heads]
            s16 = [_bf(x) for x in s]
            ws = [_dot(_bf(sol[i][:, GDN_HD:]), s16[h]) for h, i in enumerate(ids)]
            qs = [_dot(_bf(q[i] * egc[i]), s16[h]) for h, i in enumerate(ids)]
            v16 = [_bf(sol[i][:, :GDN_HD] - ws[h]) for h, i in enumerate(ids)]
            av = [_dot(_bf(qk[i] * decay[i]), v16[h]) for h, i in enumerate(ids)]
            kv = [_dot_tn(_bf(k[i] * jnp.exp(gc_last[i] - gc_col[i])), v16[h]) for h, i in enumerate(ids)]
            for h, i in enumerate(ids):
                s_ref[h] = s[h] * jnp.exp(gc_last[i]) + kv[h]
                o = qs[h] + av[h]
                o_ref[rows[i], hs[h]] = (_rms(o, gain_ref[...]) * _silu(z_ref[rows[i], hs[h]])).astype(o_ref.dtype)
        return carry

    lax.fori_loop(0, n_chunks // per_iter, body, 0)


def _gdn(conv_out, proj, z_tile, small_tile, a_log, dt_bias, gain, s0, batch, t, n_valid):
    pad = jnp.zeros((LANES - 2 * GDN_HEADS,), F32)
    alog_vec = jnp.concatenate([jnp.zeros((GDN_HEADS,), F32), a_log, pad]).reshape(1, LANES)
    bias_vec = jnp.concatenate([jnp.zeros((GDN_HEADS,), F32), dt_bias, pad]).reshape(1, LANES)
    kern = functools.partial(_gdn_kernel, n_valid=n_valid)
    vec = pl.BlockSpec((1, LANES), lambda b: (0, 0))
    st = pl.BlockSpec((None, GDN_HEADS, GDN_HD, GDN_HD), lambda b: (b, 0, 0, 0))
    return pl.pallas_call(
        kern,
        grid=(batch,),
        in_specs=[
            pl.BlockSpec((t, 3 * HEAD_W), lambda b: (b, 0)),
            pl.BlockSpec((t, HEAD_W), lambda b: (b, z_tile)),
            pl.BlockSpec((t, LANES), lambda b: (b, small_tile * (HEAD_W // LANES))),
            vec, vec, vec, st,
        ],
        out_specs=[pl.BlockSpec((t, HEAD_W), lambda b: (b, 0)), st],
        out_shape=[
            jax.ShapeDtypeStruct((batch * t, HEAD_W), BF16),
            jax.ShapeDtypeStruct((batch, GDN_HEADS, GDN_HD, GDN_HD), F32),
        ],
        compiler_params=_cparams("parallel"),
        name="gdn",
    )(conv_out, proj, proj, alog_vec, bias_vec, gain.reshape(1, LANES), s0)


_DT_LANE = 2 * GDN_HEADS


def _ssd_kernel(x_ref, bc_ref, z_ref, sm_ref, alog_ref, bias_ref, d_ref, gain_ref, h0_ref,
                y_ref, h_ref, *, n_valid):
    t = x_ref.shape[0]
    h_ref[...] = h0_ref[...]
    r, c = _chunk_iotas()
    incl = r >= c
    neg_a = -jnp.exp(alog_ref[...])
    lane = lax.broadcasted_iota(jnp.int32, (CHUNK, LANES), 1)
    first = lane < SSM_HD
    srow = lax.broadcasted_iota(jnp.int32, (LANES, 1), 0) < SSM_HD
    pairs_per_group = SSM_HEADS // SSM_GROUPS // 2

    def body(n, carry):
        r0 = pl.multiple_of(n * CHUNK, CHUNK)
        rows = pl.ds(r0, CHUNK)
        dt_all = _softplus(sm_ref[rows, :] + bias_ref[...])
        if n_valid < t:
            valid = (lax.broadcasted_iota(jnp.int32, (CHUNK, 1), 0) + r0) < n_valid
            dt_all = jnp.where(valid, dt_all, 0.0)
        dta_all = dt_all * neg_a
        groups = range(SSM_GROUPS)
        pairs = range(SSM_GROUPS * pairs_per_group)
        bg = [_bf(bc_ref[rows, g * SSM_STATE:(g + 1) * SSM_STATE]) for g in groups]
        cg = [_bf(bc_ref[rows, (SSM_GROUPS + g) * SSM_STATE:(SSM_GROUPS + g + 1) * SSM_STATE]) for g in groups]
        cb = [_dot_nt(cg[g], bg[g]) for g in groups]
        ps = [slice(p * LANES, (p + 1) * LANES) for p in pairs]
        xp = [x_ref[rows, ps[p]] for p in pairs]
        hp = [h_ref[p] for p in pairs]
        y_state = [_dot_nt(cg[p // pairs_per_group], _bf(hp[p])) for p in pairs]
        head = []
        for hh in range(SSM_HEADS):
            dt = dt_all[:, _DT_LANE + hh:_DT_LANE + hh + 1]
            ac_col, ac_row = _chunk_cumsum(dta_all[:, _DT_LANE + hh:_DT_LANE + hh + 1], r, c)
            head.append((dt, ac_col, jnp.exp(jnp.where(incl, ac_col - ac_row, -jnp.inf))))
        xdt, x_dec, e_ac, e_last = [], [], [], []
        for p in pairs:
            (dt_a, ac_a, _), (dt_b, ac_b, _) = head[2 * p], head[2 * p + 1]
            last_a = ac_a[CHUNK - 1:CHUNK, :]
            last_b = ac_b[CHUNK - 1:CHUNK, :]
            xdt.append(xp[p] * jnp.where(first, dt_a, dt_b))
            x_dec.append(xdt[p] * jnp.where(first, jnp.exp(last_a - ac_a), jnp.exp(last_b - ac_b)))
            e_ac.append(jnp.where(first, jnp.exp(ac_a), jnp.exp(ac_b)))
            e_last.append(jnp.where(srow, jnp.exp(last_a), jnp.exp(last_b)))
        h_add = [_dot_tn(_bf(x_dec[p]), bg[p // pairs_per_group]) for p in pairs]
        y_intra = [_dot(_bf(cb[p // pairs_per_group] * head[2 * p][2]), _bf(jnp.where(first, xdt[p], 0.0)))
                   + _dot(_bf(cb[p // pairs_per_group] * head[2 * p + 1][2]), _bf(jnp.where(first, 0.0, xdt[p])))
                   for p in pairs]
        ys = []
        for p in pairs:
            h_ref[p] = hp[p] * e_last[p] + h_add[p]
            y = y_intra[p] + y_state[p] * e_ac[p] + d_ref[:, ps[p]] * xp[p]
            ys.append(y * _silu(z_ref[rows, ps[p]]))
        for g in groups:
            mine = [p for p in pairs if p // pairs_per_group == g]
            ms = sum(jnp.sum(ys[p] * ys[p], axis=-1, keepdims=True) for p in mine) / (len(mine) * LANES)
            scale = lax.rsqrt(ms + NORM_EPS)
            for p in mine:
                y_ref[rows, ps[p]] = (ys[p] * scale * gain_ref[:, ps[p]]).astype(y_ref.dtype)
        return carry

    lax.fori_loop(0, t // CHUNK, body, 0)


def _ssd(conv_out, proj, z_tile, small_tile, a_log, dt_bias, d_skip, gain, h0, batch, t, n_valid):
    npair = SSM_HEADS // 2
    pad_l = jnp.zeros((_DT_LANE,), F32)
    pad_r = jnp.zeros((LANES - _DT_LANE - SSM_HEADS,), F32)
    alog_vec = jnp.concatenate([pad_l, a_log, pad_r]).reshape(1, LANES)
    bias_vec = jnp.concatenate([pad_l, dt_bias, pad_r]).reshape(1, LANES)
    d_vec = jnp.repeat(d_skip, SSM_HD).reshape(1, HEAD_W)
    kern = functools.partial(_ssd_kernel, n_valid=n_valid)
    vec = pl.BlockSpec((1, LANES), lambda b: (0, 0))
    wide = pl.BlockSpec((1, HEAD_W), lambda b: (0, 0))
    st = pl.BlockSpec((None, npair, LANES, SSM_STATE), lambda b: (b, 0, 0, 0))
    y, h = pl.pallas_call(
        kern,
        grid=(batch,),
        in_specs=[
            pl.BlockSpec((t, HEAD_W), lambda b: (b, 3)),
            pl.BlockSpec((t, HEAD_W), lambda b: (b, 4)),
            pl.BlockSpec((t, HEAD_W), lambda b: (b, z_tile)),
            pl.BlockSpec((t, LANES), lambda b: (b, small_tile * (HEAD_W // LANES))),
            vec, vec, wide, wide, st,
        ],
        out_specs=[pl.BlockSpec((t, HEAD_W), lambda b: (b, 0)), st],
        out_shape=[
            jax.ShapeDtypeStruct((batch * t, HEAD_W), BF16),
            jax.ShapeDtypeStruct((batch, npair, LANES, SSM_STATE), F32),
        ],
        compiler_params=_cparams("parallel"),
        name="ssd",
    )(conv_out, conv_out, proj, proj, alog_vec, bias_vec, d_vec, gain.reshape(1, HEAD_W),
      h0.reshape(batch, npair, LANES, SSM_STATE))
    return y, h.reshape(batch, SSM_HEADS, SSM_HD, SSM_STATE)


_ROWS = 8


_PAGE_ROWS = PAGE_SIZE * _ROWS
_PAGES_PER_STEP = 8


def _page_scores(q16, page16, valid, scale):
    return jnp.where(valid, _dot_nt(q16, page16) * scale, -jnp.inf)


def _page_specs(layer):
    def spec(k):
        return pl.BlockSpec((None, None, _PAGE_ROWS, LANES),
                            lambda b, j, pt: (pt[b, j * _PAGES_PER_STEP + k], layer, 0, 0))
    return [spec(k) for k in range(_PAGES_PER_STEP)]


def _moba_step_kernel(pt_ref, q_ref, kn_ref, vn_ref, *rest, n_sel, npg):
    pages = rest[:_PAGES_PER_STEP]
    o_ref, m_ref, l_ref, g_ref, acc_ref = rest[_PAGES_PER_STEP:]
    j = pl.program_id(1)
    scale = MOBA_HD ** -0.5
    lane = lax.broadcasted_iota(jnp.int32, (_ROWS, LANES), 1)

    @pl.when(j == 0)
    def _():
        m_ref[...] = jnp.full(m_ref.shape, -jnp.inf, F32)
        l_ref[...] = jnp.zeros_like(l_ref)
        g_ref[...] = jnp.zeros_like(g_ref)

    q = q_ref[...]
    q16 = _bf(q)
    col = lax.broadcasted_iota(jnp.int32, (_ROWS, _PAGE_ROWS), 1)
    sub = lax.broadcasted_iota(jnp.int32, (_ROWS, _PAGE_ROWS), 0)
    valid = (col % _ROWS) == (sub % MOBA_HEADS)
    m_all, l_all, g_all = m_ref[...], l_ref[...], g_ref[...]
    page16 = [_bf(p[...]) for p in pages]
    s = [_page_scores(q16, p16, valid, scale) for p16 in page16]
    m = [jnp.max(x, axis=-1, keepdims=True) for x in s]
    e = [jnp.exp(x - mx) for x, mx in zip(s, m)]
    acc = [_dot(_bf(pltpu.roll(x, MOBA_HEADS, 1)), p16) for x, p16 in zip(e, page16)]
    for k, page_ref in enumerate(pages):
        idx = j * _PAGES_PER_STEP + k
        acc_ref[idx] = acc[k]
        ksum = jnp.sum(page_ref[...].reshape(PAGE_SIZE, _ROWS, LANES), axis=0)
        here = lane == idx
        m_all = jnp.where(here, m[k], m_all)
        l_all = jnp.where(here, jnp.sum(e[k], axis=-1, keepdims=True), l_all)
        g_all = jnp.where(here, jnp.sum(ksum * q, axis=-1, keepdims=True), g_all)
    m_ref[...] = m_all
    l_ref[...] = l_all
    g_ref[...] = g_all

    @pl.when(j == pl.num_programs(1) - 1)
    def _():
        gb = g_all + jnp.where(lane % 2 == 0, pltpu.roll(g_all, LANES - 1, 1), pltpu.roll(g_all, 1, 1))
        gb = gb * (1.0 / MOBA_BLOCK)
        rank = jnp.zeros((_ROWS, LANES), jnp.int32)
        for jp in range(0, npg, 2):
            gcol = gb[:, jp:jp + 1]
            beats = (gcol > gb) | ((gcol == gb) & (jp // 2 < lane // 2))
            rank = rank + beats.astype(jnp.int32)
        sel = (rank < n_sel) & (lane < npg)
        s_own = jnp.sum(q16.astype(F32) * _bf(kn_ref[...]).astype(F32), axis=-1, keepdims=True) * scale
        m_tot = jnp.maximum(jnp.max(jnp.where(sel, m_all, -jnp.inf), axis=-1, keepdims=True), s_own)
        w = jnp.where(sel, jnp.exp(m_all - m_tot), 0.0)
        e_own = jnp.exp(s_own - m_tot)
        l_tot = jnp.sum(w * l_all, axis=-1, keepdims=True) + e_own
        acc = e_own * _bf(vn_ref[...]).astype(F32)
        for jj in range(npg):
            acc = acc + w[:, jj:jj + 1] * acc_ref[jj]
        o_ref[...] = acc / l_tot


def _moba_step(q8, k8, v8, cache, page_table, layer):
    rows, npg = page_table.shape
    assert npg % _PAGES_PER_STEP == 0 and npg <= LANES
    n_blocks = npg * PAGE_SIZE // MOBA_BLOCK + 1
    kern = functools.partial(_moba_step_kernel, n_sel=min(MOBA_TOPK, n_blocks), npg=npg)
    tile = pl.BlockSpec((None, _ROWS, LANES), lambda b, j, pt: (b, 0, 0))
    stat = pltpu.VMEM((_ROWS, LANES), F32)
    return pl.pallas_call(
        kern,
        grid_spec=pltpu.PrefetchScalarGridSpec(
            num_scalar_prefetch=1,
            grid=(rows, npg // _PAGES_PER_STEP),
            in_specs=[tile, tile, tile] + _page_specs(layer),
            out_specs=tile,
            scratch_shapes=[stat, stat, stat, pltpu.VMEM((npg, _ROWS, LANES), F32)],
        ),
        out_shape=jax.ShapeDtypeStruct((rows, _ROWS, LANES), F32),
        compiler_params=_cparams("parallel", "arbitrary"),
        name="moba_step",
    )(page_table, q8, k8, v8, *([cache] * _PAGES_PER_STEP))


def _diff_step_kernel(pt_ref, q_ref, kn_ref, vn_ref, lp_ref, gn_ref, *rest, lam_init):
    pages = rest[:_PAGES_PER_STEP]
    o_ref, m_ref, l_ref, acc_ref = rest[_PAGES_PER_STEP:]
    j = pl.program_id(1)
    scale = DIFF_HD ** -0.5

    @pl.when(j == 0)
    def _():
        m_ref[...] = jnp.full(m_ref.shape, -jnp.inf, F32)
        l_ref[...] = jnp.zeros_like(l_ref)
        acc_ref[...] = jnp.zeros_like(acc_ref)

    q16 = _bf(q_ref[...])
    col = lax.broadcasted_iota(jnp.int32, (_ROWS, _PAGE_ROWS), 1)
    sub = lax.broadcasted_iota(jnp.int32, (_ROWS, _PAGE_ROWS), 0)
    valid = (col % _ROWS) == (sub // 2)
    m_old, l_old = m_ref[...][:, 0:1], l_ref[...][:, 0:1]
    page16 = [_bf(p[...]) for p in pages]
    s = [_page_scores(q16, p16, valid, scale) for p16 in page16]
    m_run = m_old
    for x in s:
        m_run = jnp.maximum(m_run, jnp.max(x, axis=-1, keepdims=True))
    alpha = jnp.exp(m_old - m_run)
    e = [jnp.exp(x - m_run) for x in s]
    l_run = alpha * l_old + sum(jnp.sum(x, axis=-1, keepdims=True) for x in e)
    acc = alpha * acc_ref[...] + sum(_dot(_bf(pltpu.roll(x, DIFF_HEADS, 1)), p16) for x, p16 in zip(e, page16))
    m_ref[...] = jnp.broadcast_to(m_run, (_ROWS, LANES))
    l_ref[...] = jnp.broadcast_to(l_run, (_ROWS, LANES))
    acc_ref[...] = acc

    @pl.when(j == pl.num_programs(1) - 1)
    def _():
        lam = _diff_lambda(lp_ref, lam_init)
        s_new = jnp.sum(q16.astype(F32) * _bf(kn_ref[...]).astype(F32), axis=-1, keepdims=True) * scale
        m_new = jnp.maximum(m_run, s_new)
        alpha = jnp.exp(m_run - m_new)
        e = jnp.exp(s_new - m_new)
        o = (alpha * acc + e * _bf(vn_ref[...]).astype(F32)) / (alpha * l_run + e)
        d = o - lam * pltpu.roll(o, _ROWS - 1, 0)
        o_ref[...] = _rms(d, gn_ref[...]) * (1.0 - lam_init)


def _diff_step(q8, k8, v8, cache, page_table, layer, lam_params, norm_gain, lam_init):
    rows, npg = page_table.shape
    assert npg % _PAGES_PER_STEP == 0
    kern = functools.partial(_diff_step_kernel, lam_init=lam_init)
    tile = pl.BlockSpec((None, _ROWS, LANES), lambda b, j, pt: (b, 0, 0))
    stat = pltpu.VMEM((_ROWS, LANES), F32)
    return pl.pallas_call(
        kern,
        grid_spec=pltpu.PrefetchScalarGridSpec(
            num_scalar_prefetch=1,
            grid=(rows, npg // _PAGES_PER_STEP),
            in_specs=[tile, tile, tile,
                      pl.BlockSpec((4, DIFF_HD), lambda b, j, pt: (0, 0)),
                      pl.BlockSpec((1, LANES), lambda b, j, pt: (0, 0))] + _page_specs(layer),
            out_specs=tile,
            scratch_shapes=[stat, stat, stat],
        ),
        out_shape=jax.ShapeDtypeStruct((rows, _ROWS, LANES), F32),
        compiler_params=_cparams("parallel", "arbitrary"),
        name="diff_step",
    )(page_table, q8, k8, v8, lam_params, norm_gain.reshape(1, LANES), *([cache] * _PAGES_PER_STEP))


def _rope_tables(pos, hd, reps):
    rd = hd // 4
    half = rd // 2
    inv_freq = 1.0 / (ROPE_THETA ** (jnp.arange(half, dtype=F32) * (2.0 / rd)))
    ang = pos.astype(F32)[:, None] * inv_freq[None, :]
    cos = jnp.cos(ang)
    sin = jnp.sin(ang)
    n = pos.shape[0]
    zero = jnp.zeros((n, half), F32)
    c = jnp.concatenate([cos, cos, jnp.ones((n, hd - rd), F32)], axis=1)
    s1 = jnp.concatenate([zero, sin, jnp.zeros((n, hd - rd), F32)], axis=1)
    s2 = jnp.concatenate([-sin, zero, jnp.zeros((n, hd - rd), F32)], axis=1)
    return tuple(jnp.tile(a, (1, reps)) for a in (c, s1, s2))


def _pad_chunk(a):
    rows, n = a.shape
    return jnp.pad(a[:, None, :], ((0, 0), (0, CHUNK - 1), (0, 0))).reshape(rows * CHUNK, n)


def kernel(x_prompt, x_sample, cache_moba_kv, cache_diff_kv, state_gdn, state_ssm, state_conv, page_table, ffn1_norm, ffn1_w_gu, ffn1_w_down, mix_norm, w_in, conv_w, conv_b, diff_lambda, diff_norm, gdn_A_log, gdn_dt_bias, gdn_norm, ssm_A_log, ssm_dt_bias, ssm_D, ssm_norm, w_branch, w_out, ffn2_norm, ffn2_w_gu, ffn2_w_down, final_norm):
    bp, t, d = x_prompt.shape
    bs = x_sample.shape[0]
    depth = w_in.shape[0]
    assert d == D_MODEL and x_sample.shape[1] == 1 and t % MOBA_BLOCK == 0
    past_len = page_table.shape[1] * PAGE_SIZE
    assert past_len % MOBA_BLOCK == 0
    n_phys = cache_moba_kv.shape[0]
    cache_m = cache_moba_kv.reshape(n_phys, depth, _PAGE_ROWS, LANES)
    cache_d = cache_diff_kv.reshape(n_phys, depth, _PAGE_ROWS, LANES)
    comp_mask = (jnp.arange(LANES)[None, :] // DIFF_HD == jnp.arange(_ROWS)[:, None] % 2).astype(F32)

    tm_p = 512 if (bp * t) % 512 == 0 else MOBA_BLOCK
    tm_proj = 1024 if t % 1024 == 0 else tm_p
    pos_p = jnp.arange(t, dtype=jnp.int32)
    pos_s = jnp.full((bs,), past_len, jnp.int32)
    tables_p = _rope_tables(pos_p, MOBA_HD, 1) + _rope_tables(pos_p, DIFF_HD, 2)
    tables_s = _rope_tables(pos_s, MOBA_HD, 1) + _rope_tables(pos_s, DIFF_HD, 2)

    hp = x_prompt.reshape(bp * t, d)
    hs = x_sample.reshape(bs, d)
    outs = {k: [] for k in ("moba_p", "moba_s", "diff_p", "diff_s", "gdn_p", "gdn_s", "ssm_p", "ssm_s",
                            "conv_p", "conv_s")}
    y_p = y_s = kv_prev = None
    for l in range(depth):
        last = l == depth - 1
        lam_init = 0.8 - 0.6 * math.exp(-0.3 * l)
        wi = w_in[l]
        w_proj = jnp.concatenate([wi[:, _C_MOBA:_C_GDNB], wi[:, _C_SSMZ:_C_SSMDT], wi[:, _C_GDNB:_C_SSMZ],
                                  wi[:, _C_SSMDT:_C_GATE], jnp.zeros((d, HEAD_W - 4 * GDN_HEADS), F32)],
                                 axis=1).astype(BF16)
        w_gate = jnp.transpose(wi[:, _C_GATE:].reshape(d, N_BRANCH, d), (1, 0, 2)).astype(BF16)
        w_br = w_branch[l].astype(BF16)
        w_o = w_out[l].astype(BF16)
        wgu1 = ffn1_w_gu[l].astype(BF16)
        wd1 = ffn1_w_down[l].astype(BF16)
        wgu2 = ffn2_w_gu[l].astype(BF16)
        wd2 = ffn2_w_down[l].astype(BF16)
        post_gain = final_norm if last else ffn1_norm[l + 1]
        post_dtype = F32 if last else BF16

        def dense_in(h, tm, tm_proj, tables, rows_per_seq, **kv):
            h1, u = _ffn(h, ffn1_norm[l], wgu1, wd1, mix_norm[l], BF16, tm)
            return h1, u, _in_proj(u, w_proj, tables, tm_proj, rows_per_seq, **kv)

        def dense_out(h1, u, branches, tm):
            merged = _merge(u, branches, w_gate, w_br, tm)
            h2 = _outproj(merged, w_o, h1, tm)
            return _ffn(h2, ffn2_norm[l], wgu2, wd2, post_gain, post_dtype, tm)

        h1, u, (proj, kv_m, kv_d) = dense_in(hp, tm_p, tm_proj, tables_p, t, layer=l, depth=depth, prev_kv=kv_prev)
        kv_prev = (kv_m, kv_d)
        o_moba = _moba_prompt(proj, bp, t)
        o_diff = _diff_prompt(proj, diff_lambda[l], diff_norm[l], lam_init, bp, t)
        conv_out, conv_new = _conv_prompt(proj, conv_w[l], conv_b[l], bp, t, _P_CONV)
        o_gdn, s_new = _gdn(conv_out, proj, _P_GDNZ, _P_SMALL, gdn_A_log[l], gdn_dt_bias[l], gdn_norm[l],
                            jnp.zeros((bp, GDN_HEADS, GDN_HD, GDN_HD), F32), bp, t, t)
        y_ssm, h_new = _ssd(conv_out, proj, _P_SSMZ, _P_SMALL, ssm_A_log[l], ssm_dt_bias[l], ssm_D[l],
                            ssm_norm[l], jnp.zeros((bp, SSM_HEADS, SSM_HD, SSM_STATE), F32), bp, t, t)
        hp, y_p = dense_out(h1, u, (o_moba, o_diff, o_gdn, y_ssm), tm_p)
        outs["gdn_p"].append(s_new)
        outs["ssm_p"].append(h_new)
        outs["conv_p"].append(conv_new)

        h1, u, proj = dense_in(hs, bs, bs, tables_s, 1)
        qkv = proj[:, :_C_CONV]
        heads = qkv.reshape(bs, 6, 4, LANES)
        mq, mk, mv = (jnp.tile(heads[:, i], (1, 2, 1)) for i in range(3))
        dq, dk, dv = (jnp.repeat(heads[:, i], 2, axis=1) for i in range(3, 6))
        o_moba = _moba_step(mq, mk, mv, cache_m, page_table, l)[:, :MOBA_HEADS].reshape(bs, HEAD_W).astype(BF16)
        o_diff = _diff_step(dq * comp_mask, dk, dv, cache_d, page_table, l, diff_lambda[l], diff_norm[l], lam_init)
        o_diff = o_diff[:, ::2].reshape(bs, HEAD_W).astype(BF16)
        conv_out, buf_new = _conv_step(proj[:, _C_CONV:_P_GDNZ * HEAD_W], jnp.moveaxis(state_conv[l], 1, 0),
                                       conv_w[l], conv_b[l])
        conv_pad = _pad_chunk(conv_out)
        tail_pad = _pad_chunk(proj[:, _P_GDNZ * HEAD_W:])
        o_gdn, s_new = _gdn(conv_pad, tail_pad, 0, 2, gdn_A_log[l], gdn_dt_bias[l], gdn_norm[l],
                            state_gdn[l], bs, CHUNK, 1)
        y_ssm, h_new = _ssd(conv_pad, tail_pad, 1, 2, ssm_A_log[l], ssm_dt_bias[l], ssm_D[l], ssm_norm[l],
                            state_ssm[l], bs, CHUNK, 1)
        o_gdn = o_gdn.reshape(bs, CHUNK, HEAD_W)[:, 0]
        y_ssm = y_ssm.reshape(bs, CHUNK, HEAD_W)[:, 0]
        hs, y_s = dense_out(h1, u, (o_moba, o_diff, o_gdn, y_ssm), bs)
        outs["moba_s"].append(qkv[:, HEAD_W:3 * HEAD_W].reshape(bs, 1, 2, MOBA_HEADS, MOBA_HD))
        outs["diff_s"].append(qkv[:, _C_DIFF + HEAD_W:].reshape(bs, 1, 2, DIFF_HEADS, 2 * DIFF_HD))
        outs["gdn_s"].append(s_new)
        outs["ssm_s"].append(h_new)
        outs["conv_s"].append(jnp.moveaxis(buf_new, 0, 1))

    return (y_p.reshape(bp, t, d), y_s.reshape(bs, 1, d),
            kv_prev[0].reshape(bp, depth, t, 2, MOBA_HEADS, MOBA_HD), jnp.stack(outs["moba_s"], axis=1),
            kv_prev[1].reshape(bp, depth, t, 2, DIFF_HEADS, 2 * DIFF_HD), jnp.stack(outs["diff_s"], axis=1),
            jnp.stack(outs["gdn_p"]), jnp.stack(outs["gdn_s"]),
            jnp.stack(outs["ssm_p"]), jnp.stack(outs["ssm_s"]),
            jnp.stack(outs["conv_p"]), jnp.stack(outs["conv_s"]))
```

```python
import functools
import math

import jax
import jax.numpy as jnp
from jax import lax
from jax.experimental import pallas as pl
from jax.experimental.pallas import tpu as pltpu

F32 = jnp.float32
BF16 = jnp.bfloat16

D_MODEL = 2048
D_FF = 5632
NORM_EPS = 1e-6
ROPE_THETA = 500000.0
PAGE_SIZE = 128

MOBA_HEADS = 4
MOBA_HD = 128
MOBA_BLOCK = 256
MOBA_TOPK = 3
DIFF_HEADS = 4
DIFF_HD = 64
GDN_HEADS = 4
GDN_HD = 128
SSM_HEADS = 8
SSM_HD = 64
SSM_GROUPS = 2
SSM_STATE = 128
CHUNK = 64
CONV_W = 4
HEAD_W = 512
CONV_CH = 2560
N_BRANCH = 4

LANES = 128
V7X_VMEM_LIMIT = 56 * 1024 * 1024

_C_MOBA = 0
_C_DIFF = 1536
_C_CONV = 3072
_C_GDNZ = 5632
_C_GDNB = 6144
_C_GDNA = 6148
_C_SSMZ = 6152
_C_SSMDT = 6664
_C_GATE = 6672


def _cparams(*sem):
    return pltpu.CompilerParams(dimension_semantics=sem, vmem_limit_bytes=V7X_VMEM_LIMIT)


def _rms(x, gain):
    y = x * lax.rsqrt(jnp.mean(x * x, axis=-1, keepdims=True) + NORM_EPS)
    return y * gain


def _dot(a, b):
    return jnp.dot(a, b, preferred_element_type=F32)


def _dot_nt(a, b):
    return lax.dot_general(a, b, (((1,), (1,)), ((), ())), preferred_element_type=F32)


def _dot_tn(a, b):
    return lax.dot_general(a, b, (((0,), (0,)), ((), ())), preferred_element_type=F32)


def _softplus(x):
    return jnp.maximum(x, 0.0) + jnp.log(1.0 + jnp.exp(-jnp.abs(x)))


def _sigmoid(x):
    return 1.0 / (1.0 + jnp.exp(-x))


def _silu(x):
    return x * _sigmoid(x)


_FFN_SLICES = 2


def _ffn_kernel(x_ref, g_ref, wg_ref, wu_ref, wd_ref, pg_ref, o_ref, p_ref, xn_ref):
    f = pl.program_id(1)

    @pl.when(f == 0)
    def _():
        x = x_ref[...]
        xn_ref[...] = _rms(x, g_ref[...]).astype(BF16)
        o_ref[...] = x

    xn = xn_ref[...]
    tf = wg_ref.shape[1]
    w = tf // _FFN_SLICES
    gu = [(_dot(xn, wg_ref[:, c * w:(c + 1) * w]), _dot(xn, wu_ref[:, c * w:(c + 1) * w]))
          for c in range(_FFN_SLICES)]
    acc = None
    for c, (gate, up) in enumerate(gu):
        act = (0.5 * _silu(gate) * up).astype(BF16)
        part = _dot(act, wd_ref[c * w:(c + 1) * w, :])
        acc = part if acc is None else acc + part
    o_ref[...] += acc

    @pl.when(f == pl.num_programs(1) - 1)
    def _():
        p_ref[...] = _rms(o_ref[...], pg_ref[...]).astype(p_ref.dtype)


def _ffn(x, gain, w_gu, w_down, post_gain, post_dtype, tm, tf=512):
    m = x.shape[0]
    nf = D_FF // tf
    return pl.pallas_call(
        _ffn_kernel,
        grid=(m // tm, nf),
        in_specs=[
            pl.BlockSpec((tm, D_MODEL), lambda i, f: (i, 0)),
            pl.BlockSpec((1, D_MODEL), lambda i, f: (0, 0)),
            pl.BlockSpec((D_MODEL, tf), lambda i, f: (0, f)),
            pl.BlockSpec((D_MODEL, tf), lambda i, f: (0, f + nf)),
            pl.BlockSpec((tf, D_MODEL), lambda i, f: (f, 0)),
            pl.BlockSpec((1, D_MODEL), lambda i, f: (0, 0)),
        ],
        out_specs=[
            pl.BlockSpec((tm, D_MODEL), lambda i, f: (i, 0)),
            pl.BlockSpec((tm, D_MODEL), lambda i, f: (i, 0)),
        ],
        out_shape=[
            jax.ShapeDtypeStruct((m, D_MODEL), F32),
            jax.ShapeDtypeStruct((m, D_MODEL), post_dtype),
        ],
        scratch_shapes=[pltpu.VMEM((tm, D_MODEL), BF16)],
        compiler_params=_cparams("parallel", "arbitrary"),
        name="ffn",
    )(x, gain.reshape(1, D_MODEL), w_gu, w_gu, w_down, post_gain.reshape(1, D_MODEL))


def _rope(x, c_ref, s1_ref, s2_ref, half):
    n = x.shape[1]
    reps = n // LANES
    c = jnp.concatenate([c_ref[...]] * reps, axis=1)
    s1 = jnp.concatenate([s1_ref[...]] * reps, axis=1)
    s2 = jnp.concatenate([s2_ref[...]] * reps, axis=1)
    return x * c + pltpu.roll(x, half, 1) * s1 + pltpu.roll(x, n - half, 1) * s2


_P_CONV = 6
_P_GDNZ = 11
_P_SSMZ = 12
_P_SMALL = 13
_P_TILES = 14


_KV_ROWS = 8


def _in_proj_kernel(u_ref, w_ref, mc_ref, ms1_ref, ms2_ref, dc_ref, ds1_ref, ds2_ref, *rest, kv_rows):
    if kv_rows:
        o_ref, kvm_ref, kvd_ref = rest[-3:]
    else:
        (o_ref,) = rest[-1:]
        kvm_ref = kvd_ref = None
    j = pl.program_id(1)
    w = w_ref.shape[1] // 2
    tm = u_ref.shape[0]

    def rope_m(acc):
        return _rope(acc, mc_ref, ms1_ref, ms2_ref, MOBA_HD // 8)

    def rope_d(acc):
        return _rope(acc, dc_ref, ds1_ref, ds2_ref, DIFF_HD // 8)

    def plain(acc):
        return acc

    def tile(epilogue, kv_ref=None, kv=0):
        u = u_ref[...]
        halves = [_dot(u, w_ref[:, c * w:(c + 1) * w]) for c in range(2)]
        for c, acc in enumerate(halves):
            val = epilogue(acc)
            o_ref[:, c * w:(c + 1) * w] = val
            if kv_ref is not None:
                for hh in range(w // LANES):
                    row0 = kv * (_KV_ROWS // 2) + c * (w // LANES) + hh
                    kv_ref[pl.ds(row0, tm, stride=_KV_ROWS), :] = val[:, hh * LANES:(hh + 1) * LANES]

    cases = [(rope_m, None, 0), (rope_m, kvm_ref, 0), (plain, kvm_ref, 1),
             (rope_d, None, 0), (rope_d, kvd_ref, 0), (plain, kvd_ref, 1)]
    for jj, (epilogue, kv_ref, kv) in enumerate(cases):
        pl.when(j == jj)(functools.partial(tile, epilogue, kv_ref, kv))
    pl.when(j >= len(cases))(functools.partial(tile, plain))


def _in_proj(u, w, tables, tm, rows_per_seq, layer=None, depth=None, prev_kv=None):
    m = u.shape[0]
    tn = HEAD_W
    nt = rows_per_seq // tm if rows_per_seq >= tm else 1
    tspec = pl.BlockSpec((tm, LANES), lambda i, j: (i % nt, 0))
    in_specs = [
        pl.BlockSpec((tm, D_MODEL), lambda i, j: (i, 0)),
        pl.BlockSpec((D_MODEL, tn), lambda i, j: (0, j)),
    ] + [tspec] * 6
    out_specs = [pl.BlockSpec((tm, tn), lambda i, j: (i, j))]
    out_shape = [jax.ShapeDtypeStruct((m, _P_TILES * tn), F32)]
    args = [u, w, *tables]
    aliases = {}
    if layer is not None:
        kv_spec = pl.BlockSpec((None, None, tm * _KV_ROWS, LANES), lambda i, j: (i // nt, layer, i % nt, 0))
        kv_shape = jax.ShapeDtypeStruct((m // rows_per_seq, depth, rows_per_seq * _KV_ROWS, LANES), F32)
        out_specs += [kv_spec, kv_spec]
        out_shape += [kv_shape, kv_shape]
        if prev_kv is not None:
            aliases = {len(args): 1, len(args) + 1: 2}
            in_specs += [pl.BlockSpec(memory_space=pl.ANY)] * 2
            args += list(prev_kv)
    res = pl.pallas_call(
        functools.partial(_in_proj_kernel, kv_rows=layer is not None),
        grid=(m // tm, _P_TILES),
        in_specs=in_specs,
        out_specs=out_specs,
        out_shape=out_shape,
        input_output_aliases=aliases,
        compiler_params=_cparams("parallel", "arbitrary"),
        name="in_proj",
    )(*args)
    return res[0] if layer is None else res


def _merge_kernel(u_ref, b0_ref, b1_ref, b2_ref, b3_ref, wg_ref, wb_ref, o_ref):
    u = u_ref[...]
    acc = None
    for i, b_ref in enumerate((b0_ref, b1_ref, b2_ref, b3_ref)):
        gate = _sigmoid(_dot(u, wg_ref[i]))
        term = gate * _dot(b_ref[...], wb_ref[i])
        acc = term if acc is None else acc + term
    o_ref[...] = acc.astype(o_ref.dtype)


def _merge(u, branches, w_gate, w_branch, tm, tn=256):
    m = u.shape[0]
    bspec = pl.BlockSpec((tm, HEAD_W), lambda i, j: (i, 0))
    return pl.pallas_call(
        _merge_kernel,
        grid=(m // tm, D_MODEL // tn),
        in_specs=[pl.BlockSpec((tm, D_MODEL), lambda i, j: (i, 0)), bspec, bspec, bspec, bspec,
                  pl.BlockSpec((N_BRANCH, D_MODEL, tn), lambda i, j: (0, 0, j)),
                  pl.BlockSpec((N_BRANCH, HEAD_W, tn), lambda i, j: (0, 0, j))],
        out_specs=pl.BlockSpec((tm, tn), lambda i, j: (i, j)),
        out_shape=jax.ShapeDtypeStruct((m, D_MODEL), BF16),
        compiler_params=_cparams("parallel", "arbitrary"),
        name="merge",
    )(u, *branches, w_gate, w_branch)


def _outproj_kernel(a_ref, w_ref, h_ref, o_ref):
    o_ref[...] = h_ref[...] + _dot(a_ref[...], w_ref[...])


def _outproj(a, w, h, tm, tn=512):
    m = a.shape[0]
    return pl.pallas_call(
        _outproj_kernel,
        grid=(m // tm, D_MODEL // tn),
        in_specs=[pl.BlockSpec((tm, D_MODEL), lambda i, j: (i, 0)),
                  pl.BlockSpec((D_MODEL, tn), lambda i, j: (0, j)),
                  pl.BlockSpec((tm, tn), lambda i, j: (i, j))],
        out_specs=pl.BlockSpec((tm, tn), lambda i, j: (i, j)),
        out_shape=jax.ShapeDtypeStruct((m, D_MODEL), F32),
        compiler_params=_cparams("parallel", "arbitrary"),
        name="outproj",
    )(a, w, h)


def _split_hi_lo(x):
    hi = x.astype(BF16)
    lo = (x - hi.astype(F32)).astype(BF16)
    return hi, lo


def _moba_kernel(q_ref, k_ref, v_ref, o_ref, kb_ref, vb_ref, km_ref, *, nb, n_sel):
    qi = pl.program_id(2)
    t = k_ref.shape[0]
    bq = q_ref.shape[0]

    @pl.when(qi == 0)
    def _():
        k = k_ref[...]
        kb_ref[...] = k.astype(BF16)
        vb_ref[...] = v_ref[...].astype(BF16)
        km_ref[...] = jnp.zeros_like(km_ref)
        for n in range(nb):
            km_ref[n:n + 1, :] = jnp.mean(k[n * MOBA_BLOCK:(n + 1) * MOBA_BLOCK, :], axis=0, keepdims=True)

    q = q_ref[...]
    qh, ql = _split_hi_lo(q)
    kh, kl = _split_hi_lo(km_ref[...])
    gate = _dot_nt(qh, kh) + (_dot_nt(qh, kl) + _dot_nt(ql, kh))
    lane = lax.broadcasted_iota(jnp.int32, gate.shape, 1)
    past = lane < qi
    gate = jnp.where(past, gate, -jnp.inf)
    rank = jnp.zeros(gate.shape, jnp.int32)
    for m in range(nb):
        gm = gate[:, m:m + 1]
        beats = (gm > gate) | ((gm == gate) & (m < lane))
        rank = rank + beats.astype(jnp.int32)
    sel = (past & (rank < n_sel)).astype(F32).astype(BF16)
    q16 = q.astype(BF16)

    scale = MOBA_HD ** -0.5
    lrow = lax.broadcasted_iota(jnp.int32, (bq, bq), 0)
    lcol = lax.broadcasted_iota(jnp.int32, (bq, bq), 1)

    def attend(tk):
        tp = tk - bq
        s_own = jnp.where(lcol <= lrow, _dot_nt(q16, kb_ref[tp:tk, :]) * scale, -jnp.inf)
        m = jnp.max(s_own, axis=-1, keepdims=True)
        if tp:
            erow = lax.broadcasted_iota(jnp.int32, (LANES, tp), 0)
            ecol = lax.broadcasted_iota(jnp.int32, (LANES, tp), 1)
            expand = (ecol // MOBA_BLOCK == erow).astype(F32).astype(BF16)
            s_past = jnp.where(_dot(sel, expand) > 0.5, _dot_nt(q16, kb_ref[0:tp, :]) * scale, -jnp.inf)
            m = jnp.maximum(m, jnp.max(s_past, axis=-1, keepdims=True))
        e_own = jnp.exp(s_own - m)
        l = jnp.sum(e_own, axis=-1, keepdims=True)
        acc = _dot(e_own.astype(BF16), vb_ref[tp:tk, :])
        if tp:
            e_past = jnp.exp(s_past - m)
            l = l + jnp.sum(e_past, axis=-1, keepdims=True)
            acc = acc + _dot(e_past.astype(BF16), vb_ref[0:tp, :])
        o_ref[...] = (acc / l).astype(o_ref.dtype)

    for n in range(t // bq):
        pl.when(qi == n)(functools.partial(attend, (n + 1) * bq))


def _moba_prompt(qkv, batch, t):
    bq = MOBA_BLOCK
    nq = t // bq
    nb = t // MOBA_BLOCK
    kern = functools.partial(_moba_kernel, nb=nb, n_sel=min(MOBA_TOPK, nb))
    return pl.pallas_call(
        kern,
        grid=(batch, MOBA_HEADS, nq),
        in_specs=[
            pl.BlockSpec((bq, MOBA_HD), lambda b, h, i: (b * nq + i, h)),
            pl.BlockSpec((t, MOBA_HD), lambda b, h, i: (b, MOBA_HEADS + h)),
            pl.BlockSpec((t, MOBA_HD), lambda b, h, i: (b, 2 * MOBA_HEADS + h)),
        ],
        out_specs=pl.BlockSpec((bq, MOBA_HD), lambda b, h, i: (b * nq + i, h)),
        out_shape=jax.ShapeDtypeStruct((batch * t, HEAD_W), BF16),
        scratch_shapes=[pltpu.VMEM((t, MOBA_HD), BF16), pltpu.VMEM((t, MOBA_HD), BF16),
                        pltpu.VMEM((LANES, MOBA_HD), F32)],
        compiler_params=_cparams("parallel", "parallel", "arbitrary"),
        name="moba_prompt",
    )(qkv, qkv, qkv)


def _diff_lambda(lp_ref, lam_init):
    lp = lp_ref[...]
    a = jnp.sum(lp[0:1] * lp[1:2], axis=-1, keepdims=True)
    b = jnp.sum(lp[2:3] * lp[3:4], axis=-1, keepdims=True)
    return jnp.exp(a) - jnp.exp(b) + lam_init


def _diff_kernel(q_ref, k_ref, v_ref, lp_ref, g_ref, o_ref, kb_ref, vb_ref, *, lam_init):
    qi = pl.program_id(2)
    t = k_ref.shape[0]
    bq = q_ref.shape[0]

    @pl.when(qi == 0)
    def _():
        kb_ref[...] = k_ref[...].astype(BF16)
        vb_ref[...] = v_ref[...].astype(BF16)

    lam = _diff_lambda(lp_ref, lam_init)
    q = q_ref[...]
    lane = lax.broadcasted_iota(jnp.int32, q.shape, 1)
    q = q * (DIFF_HD ** -0.5)
    q1 = jnp.where(lane < DIFF_HD, q, 0.0).astype(BF16)
    q2 = jnp.where(lane >= DIFF_HD, q, 0.0).astype(BF16)
    lrow = lax.broadcasted_iota(jnp.int32, (bq, bq), 0)
    lcol = lax.broadcasted_iota(jnp.int32, (bq, bq), 1)

    def attend(tk):
        tp = tk - bq
        qs = (q1, q2)
        s_own = [jnp.where(lcol <= lrow, _dot_nt(qc, kb_ref[tp:tk, :]), -jnp.inf) for qc in qs]
        m = [jnp.max(s, axis=-1, keepdims=True) for s in s_own]
        if tp:
            s_past = [_dot_nt(qc, kb_ref[0:tp, :]) for qc in qs]
            m = [jnp.maximum(mx, jnp.max(s, axis=-1, keepdims=True)) for mx, s in zip(m, s_past)]
        e_own = [jnp.exp(s - mx) for s, mx in zip(s_own, m)]
        l = [jnp.sum(e, axis=-1, keepdims=True) for e in e_own]
        acc = [_dot(e.astype(BF16), vb_ref[tp:tk, :]) for e in e_own]
        if tp:
            e_past = [jnp.exp(s - mx) for s, mx in zip(s_past, m)]
            l = [x + jnp.sum(e, axis=-1, keepdims=True) for x, e in zip(l, e_past)]
            acc = [a + _dot(e.astype(BF16), vb_ref[0:tp, :]) for a, e in zip(acc, e_past)]
        o = acc[0] / l[0] - lam * (acc[1] / l[1])
        o_ref[...] = (_rms(o, g_ref[...]) * (1.0 - lam_init)).astype(o_ref.dtype)

    for n in range(t // bq):
        pl.when(qi == n)(functools.partial(attend, (n + 1) * bq))


def _diff_prompt(qkv, lam_params, norm_gain, lam_init, batch, t):
    bq = 512 if t % 512 == 0 else 256
    nq = t // bq
    c0 = _C_DIFF // LANES
    kern = functools.partial(_diff_kernel, lam_init=lam_init)
    return pl.pallas_call(
        kern,
        grid=(batch, DIFF_HEADS, nq),
        in_specs=[
            pl.BlockSpec((bq, LANES), lambda b, h, i: (b * nq + i, c0 + h)),
            pl.BlockSpec((t, LANES), lambda b, h, i: (b, c0 + DIFF_HEADS + h)),
            pl.BlockSpec((t, LANES), lambda b, h, i: (b, c0 + 2 * DIFF_HEADS + h)),
            pl.BlockSpec((4, DIFF_HD), lambda b, h, i: (0, 0)),
            pl.BlockSpec((1, LANES), lambda b, h, i: (0, 0)),
        ],
        out_specs=pl.BlockSpec((bq, LANES), lambda b, h, i: (b * nq + i, h)),
        out_shape=jax.ShapeDtypeStruct((batch * t, HEAD_W), BF16),
        scratch_shapes=[pltpu.VMEM((t, LANES), BF16), pltpu.VMEM((t, LANES), BF16)],
        compiler_params=_cparams("parallel", "parallel", "arbitrary"),
        name="diff_prompt",
    )(qkv, qkv, qkv, lam_params, norm_gain.reshape(1, LANES))


def _l2norm_heads(x):
    parts = []
    for h in range(x.shape[1] // LANES):
        xh = x[:, h * LANES:(h + 1) * LANES]
        parts.append(xh * lax.rsqrt(jnp.sum(xh * xh, axis=-1, keepdims=True) + 1e-6))
    return jnp.concatenate(parts, axis=1)


def _conv_kernel(x_ref, w_ref, b_ref, o_ref, tail_ref):
    c = pl.program_id(1)
    x = x_ref[...]
    t = x.shape[0]
    row = lax.broadcasted_iota(jnp.int32, x.shape, 0)
    w = w_ref[...]
    acc = b_ref[...] + w[CONV_W - 1:CONV_W] * x
    for d in range(1, CONV_W):
        shifted = jnp.where(row >= d, pltpu.roll(x, d, 0), 0.0)
        acc = acc + w[CONV_W - 1 - d:CONV_W - d] * shifted
    y = _silu(acc)
    tail_ref[...] = x[t - (CONV_W - 1):, :]

    @pl.when(c < 2)
    def _():
        o_ref[...] = _l2norm_heads(y)

    @pl.when(c >= 2)
    def _():
        o_ref[...] = y


def _conv_prompt(x, w, b, batch, t, tile0):
    tc = HEAD_W
    return pl.pallas_call(
        _conv_kernel,
        grid=(batch, CONV_CH // tc),
        in_specs=[
            pl.BlockSpec((t, tc), lambda bi, c: (bi, tile0 + c)),
            pl.BlockSpec((CONV_W, tc), lambda bi, c: (0, c)),
            pl.BlockSpec((1, tc), lambda bi, c: (0, c)),
        ],
        out_specs=[
            pl.BlockSpec((t, tc), lambda bi, c: (bi, c)),
            pl.BlockSpec((None, CONV_W - 1, tc), lambda bi, c: (bi, 0, c)),
        ],
        out_shape=[
            jax.ShapeDtypeStruct((batch * t, CONV_CH), F32),
            jax.ShapeDtypeStruct((batch, CONV_W - 1, CONV_CH), F32),
        ],
        compiler_params=_cparams("parallel", "parallel"),
        name="conv_prompt",
    )(x, w, b.reshape(1, CONV_CH))


def _conv_step_kernel(x_ref, buf_ref, w_ref, b_ref, o_ref, nb_ref):
    x = x_ref[...]
    w = w_ref[...]
    acc = b_ref[...] + w[CONV_W - 1:CONV_W] * x
    for j in range(CONV_W - 1):
        acc = acc + w[j:j + 1] * buf_ref[j]
    y = _silu(acc)
    qk = _l2norm_heads(y[:, :2 * HEAD_W])
    o_ref[...] = jnp.concatenate([qk, y[:, 2 * HEAD_W:]], axis=1)
    for j in range(CONV_W - 2):
        nb_ref[j] = buf_ref[j + 1]
    nb_ref[CONV_W - 2] = x


def _conv_step(x, buf, w, b):
    rows = x.shape[0]
    return pl.pallas_call(
        _conv_step_kernel,
        out_shape=[
            jax.ShapeDtypeStruct((rows, CONV_CH), F32),
            jax.ShapeDtypeStruct((CONV_W - 1, rows, CONV_CH), F32),
        ],
        compiler_params=pltpu.CompilerParams(vmem_limit_bytes=V7X_VMEM_LIMIT),
        name="conv_step",
    )(x, buf, w, b.reshape(1, CONV_CH))


def _chunk_iotas():
    r = lax.broadcasted_iota(jnp.int32, (CHUNK, CHUNK), 0)
    c = lax.broadcasted_iota(jnp.int32, (CHUNK, CHUNK), 1)
    return r, c


def _chunk_cumsum(col, r, c):
    row = jnp.sum(jnp.where(r == c, col, 0.0), axis=0, keepdims=True)
    cs_col = jnp.sum(jnp.where(c <= r, row, 0.0), axis=1, keepdims=True)
    cs_row = jnp.sum(jnp.where(r <= c, col, 0.0), axis=0, keepdims=True)
    return cs_col, cs_row


def _mm3(a, b):
    ah, al = a
    bh, bl = b
    return _dot(ah, bh) + (_dot(ah, bl) + _dot(al, bh))


_INV_BASE = 8


def _unit_lower_inverses(lows, r, c):
    def same_block(n):
        return (r // n) == (c // n)

    eye = (r == c).astype(F32)
    d1 = [jnp.where(same_block(_INV_BASE), low, 0.0) for low in lows]
    s1 = [_split_hi_lo(x) for x in d1]
    d2 = [_mm3(a, a) for a in s1]
    s2 = [_split_hi_lo(x) for x in d2]
    d3 = [_mm3(a, b) for a, b in zip(s1, s2)]
    d4 = [_mm3(b, b) for b in s2]
    inv = [eye - a + b - x for a, b, x in zip(d1, d2, d3)]
    inv = [x + _mm3(_split_hi_lo(x), _split_hi_lo(y)) for x, y in zip(inv, d4)]
    n = _INV_BASE
    while n < CHUNK:
        join = same_block(2 * n) & jnp.logical_not(same_block(n))
        si = [_split_hi_lo(x) for x in inv]
        mid = [_mm3(a, _split_hi_lo(jnp.where(join, low, 0.0))) for a, low in zip(si, lows)]
        inv = [x - _mm3(_split_hi_lo(y), a) for x, y, a in zip(inv, mid, si)]
        n *= 2
    return inv


def _bf(x):
    return x.astype(BF16)


def _gdn_kernel(qkv_ref, z_ref, sm_ref, alog_ref, bias_ref, gain_ref, s0_ref, o_ref, s_ref, *, n_valid):
    t = qkv_ref.shape[0]
    s_ref[...] = s0_ref[...]
    r, c = _chunk_iotas()
    incl = r >= c
    strict = r > c
    neg_a = -jnp.exp(alog_ref[...])

    n_chunks = t // CHUNK
    per_iter = 2 if n_chunks % 2 == 0 else 1
    heads = range(GDN_HEADS)
    hs = [slice(h * GDN_HD, (h + 1) * GDN_HD) for h in heads]

    def body(n, carry):
        rows, q, k, v, beta, gc_col, gc_last, decay, kb, k16 = ([] for _ in range(10))
        for ci in range(per_iter):
            r0 = pl.multiple_of((n * per_iter + ci) * CHUNK, CHUNK)
            rw = pl.ds(r0, CHUNK)
            sm = sm_ref[rw, :]
            beta_all = _sigmoid(sm)
            g_all = neg_a * _softplus(sm + bias_ref[...])
            if n_valid < t:
                valid = (lax.broadcasted_iota(jnp.int32, (CHUNK, 1), 0) + r0) < n_valid
                beta_all = jnp.where(valid, beta_all, 0.0)
                g_all = jnp.where(valid, g_all, 0.0)
            for h in heads:
                rows.append(rw)
                q.append(qkv_ref[rw, hs[h]] * (GDN_HD ** -0.5))
                k.append(qkv_ref[rw, HEAD_W + h * GDN_HD:HEAD_W + (h + 1) * GDN_HD])
                v.append(qkv_ref[rw, 2 * HEAD_W + h * GDN_HD:2 * HEAD_W + (h + 1) * GDN_HD])
                beta.append(beta_all[:, h:h + 1])
                col, row = _chunk_cumsum(g_all[:, GDN_HEADS + h:GDN_HEADS + h + 1], r, c)
                gc_col.append(col)
                gc_last.append(col[CHUNK - 1:CHUNK, :])
                decay.append(jnp.exp(jnp.where(incl, col - row, -jnp.inf)))
                kb.append(k[-1] * beta[-1])
                k16.append(_bf(k[-1]))
        every = range(per_iter * GDN_HEADS)
        kk = [_dot_nt(_bf(kb[i]), k16[i]) for i in every]
        qk = [_dot_nt(_bf(q[i]), k16[i]) for i in every]
        inv = _unit_lower_inverses([jnp.where(strict, kk[i] * decay[i], 0.0) for i in every], r, c)
        egc = [jnp.exp(x) for x in gc_col]
        rhs = [jnp.concatenate([v[i] * beta[i], kb[i] * egc[i]], axis=1) for i in every]
        sol = [_dot(_bf(inv[i]), _bf(rhs[i])) for i in every]
        for ci in range(per_iter):
            ids = [ci * GDN_HEADS + h for h in heads]
            s = [s_ref[h] for h in heads]
            s16 = [_bf(x) for x in s]
            ws = [_dot(_bf(sol[i][:, GDN_HD:]), s16[h]) for h, i in enumerate(ids)]
            qs = [_dot(_bf(q[i] * egc[i]), s16[h]) for h, i in enumerate(ids)]
            v16 = [_bf(sol[i][:, :GDN_HD] - ws[h]) for h, i in enumerate(ids)]
            av = [_dot(_bf(qk[i] * decay[i]), v16[h]) for h, i in enumerate(ids)]
            kv = [_dot_tn(_bf(k[i] * jnp.exp(gc_last[i] - gc_col[i])), v16[h]) for h, i in enumerate(ids)]
            for h, i in enumerate(ids):
                s_ref[h] = s[h] * jnp.exp(gc_last[i]) + kv[h]
                o = qs[h] + av[h]
                o_ref[rows[i], hs[h]] = (_rms(o, gain_ref[...]) * _silu(z_ref[rows[i], hs[h]])).astype(o_ref.dtype)
        return carry

    lax.fori_loop(0, n_chunks // per_iter, body, 0)


def _gdn(conv_out, proj, z_tile, small_tile, a_log, dt_bias, gain, s0, batch, t, n_valid):
    pad = jnp.zeros((LANES - 2 * GDN_HEADS,), F32)
    alog_vec = jnp.concatenate([jnp.zeros((GDN_HEADS,), F32), a_log, pad]).reshape(1, LANES)
    bias_vec = jnp.concatenate([jnp.zeros((GDN_HEADS,), F32), dt_bias, pad]).reshape(1, LANES)
    kern = functools.partial(_gdn_kernel, n_valid=n_valid)
    vec = pl.BlockSpec((1, LANES), lambda b: (0, 0))
    st = pl.BlockSpec((None, GDN_HEADS, GDN_HD, GDN_HD), lambda b: (b, 0, 0, 0))
    return pl.pallas_call(
        kern,
        grid=(batch,),
        in_specs=[
            pl.BlockSpec((t, 3 * HEAD_W), lambda b: (b, 0)),
            pl.BlockSpec((t, HEAD_W), lambda b: (b, z_tile)),
            pl.BlockSpec((t, LANES), lambda b: (b, small_tile * (HEAD_W // LANES))),
            vec, vec, vec, st,
        ],
        out_specs=[pl.BlockSpec((t, HEAD_W), lambda b: (b, 0)), st],
        out_shape=[
            jax.ShapeDtypeStruct((batch * t, HEAD_W), BF16),
            jax.ShapeDtypeStruct((batch, GDN_HEADS, GDN_HD, GDN_HD), F32),
        ],
        compiler_params=_cparams("parallel"),
        name="gdn",
    )(conv_out, proj, proj, alog_vec, bias_vec, gain.reshape(1, LANES), s0)


_DT_LANE = 2 * GDN_HEADS


def _ssd_kernel(x_ref, bc_ref, z_ref, sm_ref, alog_ref, bias_ref, d_ref, gain_ref, h0_ref,
                y_ref, h_ref, *, n_valid):
    t = x_ref.shape[0]
    h_ref[...] = h0_ref[...]
    r, c = _chunk_iotas()
    incl = r >= c
    neg_a = -jnp.exp(alog_ref[...])
    lane = lax.broadcasted_iota(jnp.int32, (CHUNK, LANES), 1)
    first = lane < SSM_HD
    srow = lax.broadcasted_iota(jnp.int32, (LANES, 1), 0) < SSM_HD
    pairs_per_group = SSM_HEADS // SSM_GROUPS // 2

    def body(n, carry):
        r0 = pl.multiple_of(n * CHUNK, CHUNK)
        rows = pl.ds(r0, CHUNK)
        dt_all = _softplus(sm_ref[rows, :] + bias_ref[...])
        if n_valid < t:
            valid = (lax.broadcasted_iota(jnp.int32, (CHUNK, 1), 0) + r0) < n_valid
            dt_all = jnp.where(valid, dt_all, 0.0)
        dta_all = dt_all * neg_a
        groups = range(SSM_GROUPS)
        pairs = range(SSM_GROUPS * pairs_per_group)
        bg = [_bf(bc_ref[rows, g * SSM_STATE:(g + 1) * SSM_STATE]) for g in groups]
        cg = [_bf(bc_ref[rows, (SSM_GROUPS + g) * SSM_STATE:(SSM_GROUPS + g + 1) * SSM_STATE]) for g in groups]
        cb = [_dot_nt(cg[g], bg[g]) for g in groups]
        ps = [slice(p * LANES, (p + 1) * LANES) for p in pairs]
        xp = [x_ref[rows, ps[p]] for p in pairs]
        hp = [h_ref[p] for p in pairs]
        y_state = [_dot_nt(cg[p // pairs_per_group], _bf(hp[p])) for p in pairs]
        head = []
        for hh in range(SSM_HEADS):
            dt = dt_all[:, _DT_LANE + hh:_DT_LANE + hh + 1]
            ac_col, ac_row = _chunk_cumsum(dta_all[:, _DT_LANE + hh:_DT_LANE + hh + 1], r, c)
            head.append((dt, ac_col, jnp.exp(jnp.where(incl, ac_col - ac_row, -jnp.inf))))
        xdt, x_dec, e_ac, e_last = [], [], [], []
        for p in pairs:
            (dt_a, ac_a, _), (dt_b, ac_b, _) = head[2 * p], head[2 * p + 1]
            last_a = ac_a[CHUNK - 1:CHUNK, :]
            last_b = ac_b[CHUNK - 1:CHUNK, :]
            xdt.append(xp[p] * jnp.where(first, dt_a, dt_b))
            x_dec.append(xdt[p] * jnp.where(first, jnp.exp(last_a - ac_a), jnp.exp(last_b - ac_b)))
            e_ac.append(jnp.where(first, jnp.exp(ac_a), jnp.exp(ac_b)))
            e_last.append(jnp.where(srow, jnp.exp(last_a), jnp.exp(last_b)))
        h_add = [_dot_tn(_bf(x_dec[p]), bg[p // pairs_per_group]) for p in pairs]
        y_intra = [_dot(_bf(cb[p // pairs_per_group] * head[2 * p][2]), _bf(jnp.where(first, xdt[p], 0.0)))
                   + _dot(_bf(cb[p // pairs_per_group] * head[2 * p + 1][2]), _bf(jnp.where(first, 0.0, xdt[p])))
                   for p in pairs]
        ys = []
        for p in pairs:
            h_ref[p] = hp[p] * e_last[p] + h_add[p]
            y = y_intra[p] + y_state[p] * e_ac[p] + d_ref[:, ps[p]] * xp[p]
            ys.append(y * _silu(z_ref[rows, ps[p]]))
        for g in groups:
            mine = [p for p in pairs if p // pairs_per_group == g]
            ms = sum(jnp.sum(ys[p] * ys[p], axis=-1, keepdims=True) for p in mine) / (len(mine) * LANES)
            scale = lax.rsqrt(ms + NORM_EPS)
            for p in mine:
                y_ref[rows, ps[p]] = (ys[p] * scale * gain_ref[:, ps[p]]).astype(y_ref.dtype)
        return carry

    lax.fori_loop(0, t // CHUNK, body, 0)


def _ssd(conv_out, proj, z_tile, small_tile, a_log, dt_bias, d_skip, gain, h0, batch, t, n_valid):
    npair = SSM_HEADS // 2
    pad_l = jnp.zeros((_DT_LANE,), F32)
    pad_r = jnp.zeros((LANES - _DT_LANE - SSM_HEADS,), F32)
    alog_vec = jnp.concatenate([pad_l, a_log, pad_r]).reshape(1, LANES)
    bias_vec = jnp.concatenate([pad_l, dt_bias, pad_r]).reshape(1, LANES)
    d_vec = jnp.repeat(d_skip, SSM_HD).reshape(1, HEAD_W)
    kern = functools.partial(_ssd_kernel, n_valid=n_valid)
    vec = pl.BlockSpec((1, LANES), lambda b: (0, 0))
    wide = pl.BlockSpec((1, HEAD_W), lambda b: (0, 0))
    st = pl.BlockSpec((None, npair, LANES, SSM_STATE), lambda b: (b, 0, 0, 0))
    y, h = pl.pallas_call(
        kern,
        grid=(batch,),
        in_specs=[
            pl.BlockSpec((t, HEAD_W), lambda b: (b, 3)),
            pl.BlockSpec((t, HEAD_W), lambda b: (b, 4)),
            pl.BlockSpec((t, HEAD_W), lambda b: (b, z_tile)),
            pl.BlockSpec((t, LANES), lambda b: (b, small_tile * (HEAD_W // LANES))),
            vec, vec, wide, wide, st,
        ],
        out_specs=[pl.BlockSpec((t, HEAD_W), lambda b: (b, 0)), st],
        out_shape=[
            jax.ShapeDtypeStruct((batch * t, HEAD_W), BF16),
            jax.ShapeDtypeStruct((batch, npair, LANES, SSM_STATE), F32),
        ],
        compiler_params=_cparams("parallel"),
        name="ssd",
    )(conv_out, conv_out, proj, proj, alog_vec, bias_vec, d_vec, gain.reshape(1, HEAD_W),
      h0.reshape(batch, npair, LANES, SSM_STATE))
    return y, h.reshape(batch, SSM_HEADS, SSM_HD, SSM_STATE)


_ROWS = 8


_PAGE_ROWS = PAGE_SIZE * _ROWS
_PAGES_PER_STEP = 16


def _page_scores(q16, page16, valid, scale):
    return jnp.where(valid, _dot_nt(q16, page16) * scale, -jnp.inf)


def _page_specs(layer):
    def spec(k):
        return pl.BlockSpec((None, None, _PAGE_ROWS, LANES),
                            lambda b, j, pt: (pt[b, j * _PAGES_PER_STEP + k], layer, 0, 0))
    return [spec(k) for k in range(_PAGES_PER_STEP)]


def _moba_step_kernel(pt_ref, q_ref, kn_ref, vn_ref, *rest, n_sel, npg):
    pages = rest[:_PAGES_PER_STEP]
    o_ref, m_ref, l_ref, g_ref, acc_ref = rest[_PAGES_PER_STEP:]
    j = pl.program_id(1)
    scale = MOBA_HD ** -0.5
    lane = lax.broadcasted_iota(jnp.int32, (_ROWS, LANES), 1)

    @pl.when(j == 0)
    def _():
        m_ref[...] = jnp.full(m_ref.shape, -jnp.inf, F32)
        l_ref[...] = jnp.zeros_like(l_ref)
        g_ref[...] = jnp.zeros_like(g_ref)

    q = q_ref[...]
    q16 = _bf(q)
    col = lax.broadcasted_iota(jnp.int32, (_ROWS, _PAGE_ROWS), 1)
    sub = lax.broadcasted_iota(jnp.int32, (_ROWS, _PAGE_ROWS), 0)
    valid = (col % _ROWS) == (sub % MOBA_HEADS)
    m_all, l_all, g_all = m_ref[...], l_ref[...], g_ref[...]
    page16 = [_bf(p[...]) for p in pages]
    s = [_page_scores(q16, p16, valid, scale) for p16 in page16]
    m = [jnp.max(x, axis=-1, keepdims=True) for x in s]
    e = [jnp.exp(x - mx) for x, mx in zip(s, m)]
    acc = [_dot(_bf(pltpu.roll(x, MOBA_HEADS, 1)), p16) for x, p16 in zip(e, page16)]
    for k, page_ref in enumerate(pages):
        idx = j * _PAGES_PER_STEP + k
        acc_ref[idx] = acc[k]
        ksum = jnp.sum(page_ref[...].reshape(PAGE_SIZE, _ROWS, LANES), axis=0)
        here = lane == idx
        m_all = jnp.where(here, m[k], m_all)
        l_all = jnp.where(here, jnp.sum(e[k], axis=-1, keepdims=True), l_all)
        g_all = jnp.where(here, jnp.sum(ksum * q, axis=-1, keepdims=True), g_all)
    m_ref[...] = m_all
    l_ref[...] = l_all
    g_ref[...] = g_all

    @pl.when(j == pl.num_programs(1) - 1)
    def _():
        gb = g_all + jnp.where(lane % 2 == 0, pltpu.roll(g_all, LANES - 1, 1), pltpu.roll(g_all, 1, 1))
        gb = gb * (1.0 / MOBA_BLOCK)
        rank = jnp.zeros((_ROWS, LANES), jnp.int32)
        for jp in range(0, npg, 2):
            gcol = gb[:, jp:jp + 1]
            beats = (gcol > gb) | ((gcol == gb) & (jp // 2 < lane // 2))
            rank = rank + beats.astype(jnp.int32)
        sel = (rank < n_sel) & (lane < npg)
        s_own = jnp.sum(q16.astype(F32) * _bf(kn_ref[...]).astype(F32), axis=-1, keepdims=True) * scale
        m_tot = jnp.maximum(jnp.max(jnp.where(sel, m_all, -jnp.inf), axis=-1, keepdims=True), s_own)
        w = jnp.where(sel, jnp.exp(m_all - m_tot), 0.0)
        e_own = jnp.exp(s_own - m_tot)
        l_tot = jnp.sum(w * l_all, axis=-1, keepdims=True) + e_own
        acc = e_own * _bf(vn_ref[...]).astype(F32)
        for jj in range(npg):
            acc = acc + w[:, jj:jj + 1] * acc_ref[jj]
        o_ref[...] = acc / l_tot


def _moba_step(q8, k8, v8, cache, page_table, layer):
    rows, npg = page_table.shape
    assert npg % _PAGES_PER_STEP == 0 and npg <= LANES
    n_blocks = npg * PAGE_SIZE // MOBA_BLOCK + 1
    kern = functools.partial(_moba_step_kernel, n_sel=min(MOBA_TOPK, n_blocks), npg=npg)
    tile = pl.BlockSpec((None, _ROWS, LANES), lambda b, j, pt: (b, 0, 0))
    stat = pltpu.VMEM((_ROWS, LANES), F32)
    return pl.pallas_call(
        kern,
        grid_spec=pltpu.PrefetchScalarGridSpec(
            num_scalar_prefetch=1,
            grid=(rows, npg // _PAGES_PER_STEP),
            in_specs=[tile, tile, tile] + _page_specs(layer),
            out_specs=tile,
            scratch_shapes=[stat, stat, stat, pltpu.VMEM((npg, _ROWS, LANES), F32)],
        ),
        out_shape=jax.ShapeDtypeStruct((rows, _ROWS, LANES), F32),
        compiler_params=_cparams("parallel", "arbitrary"),
        name="moba_step",
    )(page_table, q8, k8, v8, *([cache] * _PAGES_PER_STEP))


def _diff_step_kernel(pt_ref, q_ref, kn_ref, vn_ref, lp_ref, gn_ref, *rest, lam_init):
    pages = rest[:_PAGES_PER_STEP]
    o_ref, m_ref, l_ref, acc_ref = rest[_PAGES_PER_STEP:]
    j = pl.program_id(1)
    scale = DIFF_HD ** -0.5

    @pl.when(j == 0)
    def _():
        m_ref[...] = jnp.full(m_ref.shape, -jnp.inf, F32)
        l_ref[...] = jnp.zeros_like(l_ref)
        acc_ref[...] = jnp.zeros_like(acc_ref)

    q16 = _bf(q_ref[...])
    col = lax.broadcasted_iota(jnp.int32, (_ROWS, _PAGE_ROWS), 1)
    sub = lax.broadcasted_iota(jnp.int32, (_ROWS, _PAGE_ROWS), 0)
    valid = (col % _ROWS) == (sub // 2)
    m_old, l_old = m_ref[...][:, 0:1], l_ref[...][:, 0:1]
    page16 = [_bf(p[...]) for p in pages]
    s = [_page_scores(q16, p16, valid, scale) for p16 in page16]
    m_run = m_old
    for x in s:
        m_run = jnp.maximum(m_run, jnp.max(x, axis=-1, keepdims=True))
    alpha = jnp.exp(m_old - m_run)
    e = [jnp.exp(x - m_run) for x in s]
    l_run = alpha * l_old + sum(jnp.sum(x, axis=-1, keepdims=True) for x in e)
    acc = alpha * acc_ref[...] + sum(_dot(_bf(pltpu.roll(x, DIFF_HEADS, 1)), p16) for x, p16 in zip(e, page16))
    m_ref[...] = jnp.broadcast_to(m_run, (_ROWS, LANES))
    l_ref[...] = jnp.broadcast_to(l_run, (_ROWS, LANES))
    acc_ref[...] = acc

    @pl.when(j == pl.num_programs(1) - 1)
    def _():
        lam = _diff_lambda(lp_ref, lam_init)
        s_new = jnp.sum(q16.astype(F32) * _bf(kn_ref[...]).astype(F32), axis=-1, keepdims=True) * scale
        m_new = jnp.maximum(m_run, s_new)
        alpha = jnp.exp(m_run - m_new)
        e = jnp.exp(s_new - m_new)
        o = (alpha * acc + e * _bf(vn_ref[...]).astype(F32)) / (alpha * l_run + e)
        d = o - lam * pltpu.roll(o, _ROWS - 1, 0)
        o_ref[...] = _rms(d, gn_ref[...]) * (1.0 - lam_init)


def _diff_step(q8, k8, v8, cache, page_table, layer, lam_params, norm_gain, lam_init):
    rows, npg = page_table.shape
    assert npg % _PAGES_PER_STEP == 0
    kern = functools.partial(_diff_step_kernel, lam_init=lam_init)
    tile = pl.BlockSpec((None, _ROWS, LANES), lambda b, j, pt: (b, 0, 0))
    stat = pltpu.VMEM((_ROWS, LANES), F32)
    return pl.pallas_call(
        kern,
        grid_spec=pltpu.PrefetchScalarGridSpec(
            num_scalar_prefetch=1,
            grid=(rows, npg // _PAGES_PER_STEP),
            in_specs=[tile, tile, tile,
                      pl.BlockSpec((4, DIFF_HD), lambda b, j, pt: (0, 0)),
                      pl.BlockSpec((1, LANES), lambda b, j, pt: (0, 0))] + _page_specs(layer),
            out_specs=tile,
            scratch_shapes=[stat, stat, stat],
        ),
        out_shape=jax.ShapeDtypeStruct((rows, _ROWS, LANES), F32),
        compiler_params=_cparams("parallel", "arbitrary"),
        name="diff_step",
    )(page_table, q8, k8, v8, lam_params, norm_gain.reshape(1, LANES), *([cache] * _PAGES_PER_STEP))


def _rope_tables(pos, hd, reps):
    rd = hd // 4
    half = rd // 2
    inv_freq = 1.0 / (ROPE_THETA ** (jnp.arange(half, dtype=F32) * (2.0 / rd)))
    ang = pos.astype(F32)[:, None] * inv_freq[None, :]
    cos = jnp.cos(ang)
    sin = jnp.sin(ang)
    n = pos.shape[0]
    zero = jnp.zeros((n, half), F32)
    c = jnp.concatenate([cos, cos, jnp.ones((n, hd - rd), F32)], axis=1)
    s1 = jnp.concatenate([zero, sin, jnp.zeros((n, hd - rd), F32)], axis=1)
    s2 = jnp.concatenate([-sin, zero, jnp.zeros((n, hd - rd), F32)], axis=1)
    return tuple(jnp.tile(a, (1, reps)) for a in (c, s1, s2))


def _pad_chunk(a):
    rows, n = a.shape
    return jnp.pad(a[:, None, :], ((0, 0), (0, CHUNK - 1), (0, 0))).reshape(rows * CHUNK, n)


def kernel(x_prompt, x_sample, cache_moba_kv, cache_diff_kv, state_gdn, state_ssm, state_conv, page_table, ffn1_norm, ffn1_w_gu, ffn1_w_down, mix_norm, w_in, conv_w, conv_b, diff_lambda, diff_norm, gdn_A_log, gdn_dt_bias, gdn_norm, ssm_A_log, ssm_dt_bias, ssm_D, ssm_norm, w_branch, w_out, ffn2_norm, ffn2_w_gu, ffn2_w_down, final_norm):
    bp, t, d = x_prompt.shape
    bs = x_sample.shape[0]
    depth = w_in.shape[0]
    assert d == D_MODEL and x_sample.shape[1] == 1 and t % MOBA_BLOCK == 0
    past_len = page_table.shape[1] * PAGE_SIZE
    assert past_len % MOBA_BLOCK == 0
    n_phys = cache_moba_kv.shape[0]
    cache_m = cache_moba_kv.reshape(n_phys, depth, _PAGE_ROWS, LANES)
    cache_d = cache_diff_kv.reshape(n_phys, depth, _PAGE_ROWS, LANES)
    comp_mask = (jnp.arange(LANES)[None, :] // DIFF_HD == jnp.arange(_ROWS)[:, None] % 2).astype(F32)

    tm_p = 512 if (bp * t) % 512 == 0 else MOBA_BLOCK
    tm_proj = 1024 if t % 1024 == 0 else tm_p
    pos_p = jnp.arange(t, dtype=jnp.int32)
    pos_s = jnp.full((bs,), past_len, jnp.int32)
    tables_p = _rope_tables(pos_p, MOBA_HD, 1) + _rope_tables(pos_p, DIFF_HD, 2)
    tables_s = _rope_tables(pos_s, MOBA_HD, 1) + _rope_tables(pos_s, DIFF_HD, 2)

    hp = x_prompt.reshape(bp * t, d)
    hs = x_sample.reshape(bs, d)
    outs = {k: [] for k in ("moba_p", "moba_s", "diff_p", "diff_s", "gdn_p", "gdn_s", "ssm_p", "ssm_s",
                            "conv_p", "conv_s")}
    y_p = y_s = kv_prev = None
    for l in range(depth):
        last = l == depth - 1
        lam_init = 0.8 - 0.6 * math.exp(-0.3 * l)
        wi = w_in[l]
        w_proj = jnp.concatenate([wi[:, _C_MOBA:_C_GDNB], wi[:, _C_SSMZ:_C_SSMDT], wi[:, _C_GDNB:_C_SSMZ],
                                  wi[:, _C_SSMDT:_C_GATE], jnp.zeros((d, HEAD_W - 4 * GDN_HEADS), F32)],
                                 axis=1).astype(BF16)
        w_gate = jnp.transpose(wi[:, _C_GATE:].reshape(d, N_BRANCH, d), (1, 0, 2)).astype(BF16)
        w_br = w_branch[l].astype(BF16)
        w_o = w_out[l].astype(BF16)
        wgu1 = ffn1_w_gu[l].astype(BF16)
        wd1 = ffn1_w_down[l].astype(BF16)
        wgu2 = ffn2_w_gu[l].astype(BF16)
        wd2 = ffn2_w_down[l].astype(BF16)
        post_gain = final_norm if last else ffn1_norm[l + 1]
        post_dtype = F32 if last else BF16

        def dense_in(h, tm, tm_proj, tables, rows_per_seq, **kv):
            h1, u = _ffn(h, ffn1_norm[l], wgu1, wd1, mix_norm[l], BF16, tm)
            return h1, u, _in_proj(u, w_proj, tables, tm_proj, rows_per_seq, **kv)

        def dense_out(h1, u, branches, tm):
            merged = _merge(u, branches, w_gate, w_br, tm)
            h2 = _outproj(merged, w_o, h1, tm)
            return _ffn(h2, ffn2_norm[l], wgu2, wd2, post_gain, post_dtype, tm)

        h1, u, (proj, kv_m, kv_d) = dense_in(hp, tm_p, tm_proj, tables_p, t, layer=l, depth=depth, prev_kv=kv_prev)
        kv_prev = (kv_m, kv_d)
        o_moba = _moba_prompt(proj, bp, t)
        o_diff = _diff_prompt(proj, diff_lambda[l], diff_norm[l], lam_init, bp, t)
        conv_out, conv_new = _conv_prompt(proj, conv_w[l], conv_b[l], bp, t, _P_CONV)
        o_gdn, s_new = _gdn(conv_out, proj, _P_GDNZ, _P_SMALL, gdn_A_log[l], gdn_dt_bias[l], gdn_norm[l],
                            jnp.zeros((bp, GDN_HEADS, GDN_HD, GDN_HD), F32), bp, t, t)
        y_ssm, h_new = _ssd(conv_out, proj, _P_SSMZ, _P_SMALL, ssm_A_log[l], ssm_dt_bias[l], ssm_D[l],
                            ssm_norm[l], jnp.zeros((bp, SSM_HEADS, SSM_HD, SSM_STATE), F32), bp, t, t)
        hp, y_p = dense_out(h1, u, (o_moba, o_diff, o_gdn, y_ssm), tm_p)
        outs["gdn_p"].append(s_new)
        outs["ssm_p"].append(h_new)
        outs["conv_p"].append(conv_new)

        h1, u, proj = dense_in(hs, bs, bs, tables_s, 1)
        qkv = proj[:, :_C_CONV]
        heads = qkv.reshape(bs, 6, 4, LANES)
        mq, mk, mv = (jnp.tile(heads[:, i], (1, 2, 1)) for i in range(3))
        dq, dk, dv = (jnp.repeat(heads[:, i], 2, axis=1) for i in range(3, 6))
        o_moba = _moba_step(mq, mk, mv, cache_m, page_table, l)[:, :MOBA_HEADS].reshape(bs, HEAD_W).astype(BF16)
        o_diff = _diff_step(dq * comp_mask, dk, dv, cache_d, page_table, l, diff_lambda[l], diff_norm[l], lam_init)
        o_diff = o_diff[:, ::2].reshape(bs, HEAD_W).astype(BF16)
        conv_out, buf_new = _conv_step(proj[:, _C_CONV:_P_GDNZ * HEAD_W], jnp.moveaxis(state_conv[l], 1, 0),
                                       conv_w[l], conv_b[l])
        conv_pad = _pad_chunk(conv_out)
        tail_pad = _pad_chunk(proj[:, _P_GDNZ * HEAD_W:])
        o_gdn, s_new = _gdn(conv_pad, tail_pad, 0, 2, gdn_A_log[l], gdn_dt_bias[l], gdn_norm[l],
                            state_gdn[l], bs, CHUNK, 1)
        y_ssm, h_new = _ssd(conv_pad, tail_pad, 1, 2, ssm_A_log[l], ssm_dt_bias[l], ssm_D[l], ssm_norm[l],
                            state_ssm[l], bs, CHUNK, 1)
        o_gdn = o_gdn.reshape(bs, CHUNK, HEAD_W)[:, 0]
        y_ssm = y_ssm.reshape(bs, CHUNK, HEAD_W)[:, 0]
        hs, y_s = dense_out(h1, u, (o_moba, o_diff, o_gdn, y_ssm), bs)
        outs["moba_s"].append(qkv[:, HEAD_W:3 * HEAD_W].reshape(bs, 1, 2, MOBA_HEADS, MOBA_HD))
        outs["diff_s"].append(qkv[:, _C_DIFF + HEAD_W:].reshape(bs, 1, 2, DIFF_HEADS, 2 * DIFF_HD))
        outs["gdn_s"].append(s_new)
        outs["ssm_s"].append(h_new)
        outs["conv_s"].append(jnp.moveaxis(buf_new, 0, 1))

    return (y_p.reshape(bp, t, d), y_s.reshape(bs, 1, d),
            kv_prev[0].reshape(bp, depth, t, 2, MOBA_HEADS, MOBA_HD), jnp.stack(outs["moba_s"], axis=1),
            kv_prev[1].reshape(bp, depth, t, 2, DIFF_HEADS, 2 * DIFF_HD), jnp.stack(outs["diff_s"], axis=1),
            jnp.stack(outs["gdn_p"]), jnp.stack(outs["gdn_s"]),
            jnp.stack(outs["ssm_p"]), jnp.stack(outs["ssm_s"]),
            jnp.stack(outs["conv_p"]), jnp.stack(outs["conv_s"]))
```
